```python
import math
import jax
import jax.numpy as jnp
from jax import lax
import numpy as np

D_MODEL = 1024
BATCH = 4
SEQ = 4096
DEPTH = 1
DEC_BATCH = 128
DEC_SEQ = 1
PAST_LEN = 2048
PAGE_SIZE = 128

H_A = 4
DH_QK = 64
DV_A = 2 * DH_QK
Q_BLOCK = 128
H_B = 4
DK_B = 128
DV_B = 128
CHUNK_B = 64
N_KEYS = 128
N_EXPERTS = N_KEYS * N_KEYS
H_P = 8
DK_P = 128
DK_P_HALF = DK_P // 2
TOPK_P = 16
PEER_BLOCK = 256
EPS = 1e-6
POOL_NUM = 5
POOL_DEN = 4

IN_SIZES = (H_A * 2 * DH_QK, H_A * 2 * DH_QK, H_A * DV_A,
            H_B * DK_B, H_B * DK_B, H_B * DV_B, H_B * DV_B,
            D_MODEL, D_MODEL)
N_IN = sum(IN_SIZES)

kernel_name = 'hybrid_diffattn_hgrn2_peer_step'


def _rmsnorm(x, g):
    xf = x.astype(jnp.float32)
    y = xf * lax.rsqrt(jnp.mean(xf * xf, axis=-1, keepdims=True) + EPS)
    return (y * g.astype(jnp.float32)).astype(x.dtype)


def _split_in(proj):
    B, T = proj.shape[:2]
    offs = np.cumsum(IN_SIZES)[:-1].tolist()
    qa, ka, va, qb, fb, ib, gb, gate_a, gate_b = jnp.split(proj, offs, axis=-1)
    return (qa.reshape(B, T, H_A, 2, DH_QK), ka.reshape(B, T, H_A, 2, DH_QK),
            va.reshape(B, T, H_A, DV_A), qb.reshape(B, T, H_B, DK_B),
            fb.reshape(B, T, H_B, DK_B), ib.reshape(B, T, H_B, DV_B),
            gb.reshape(B, T, H_B, DV_B), gate_a, gate_b)


def _diff_lambda(lq1, lk1, lq2, lk2, lam_init):
    f = jnp.float32
    return (jnp.exp(jnp.sum(lq1.astype(f) * lk1.astype(f)))
            - jnp.exp(jnp.sum(lq2.astype(f) * lk2.astype(f))) + lam_init)


def _diff_attn_core(q, k, v, q_pos, k_pos, lam):
    s = jnp.einsum('bqhcd,bkhcd->bhcqk', q, k).astype(jnp.float32) * (DH_QK ** -0.5)
    s = jnp.where(k_pos[None, :] <= q_pos[:, None], s, -jnp.inf)
    p = jax.nn.softmax(s, axis=-1)
    a = p[:, :, 0] - lam * p[:, :, 1]
    return jnp.einsum('bhqk,bkhd->bqhd', a.astype(v.dtype), v)


def _diff_attn_prompt(q, k, v, lam):
    B, T = q.shape[:2]
    nb = T // Q_BLOCK
    qb = q.reshape(B, nb, Q_BLOCK, H_A, 2, DH_QK).swapaxes(0, 1)
    k_pos = jnp.arange(T)

    def one_block(args):
        qi, bi = args
        q_pos = bi * Q_BLOCK + jnp.arange(Q_BLOCK)
        return _diff_attn_core(qi, k, v, q_pos, k_pos, lam)

    out = lax.map(one_block, (qb, jnp.arange(nb)))
    return out.swapaxes(0, 1).reshape(B, T, H_A, DV_A)


def _diff_post(o, subln_w, lam_init):
    B, T = o.shape[:2]
    return (_rmsnorm(o, subln_w) * (1.0 - lam_init)).reshape(B, T, H_A * DV_A)


def _hgrn_inputs(qb, fb, lb):
    f32 = jnp.float32
    q = jax.nn.silu(qb.astype(f32)) * (DK_B ** -0.5)
    f = lb + (1.0 - lb) * jax.nn.sigmoid(fb.astype(f32))
    return q, 1.0 - f, jnp.log(f)


def _hgrn_prompt(q, k, v, logf):
    B, T = q.shape[:2]
    nc = T // CHUNK_B

    def chunks(a):
        return a.reshape(B, nc, CHUNK_B, H_B, a.shape[-1]).transpose(1, 0, 3, 2, 4)

    causal = jnp.tril(jnp.ones((CHUNK_B, CHUNK_B), dtype=bool))[None, None, :, :, None]

    def step(S, xs):
        qc, kc, vc, lc = xs
        g = jnp.cumsum(lc, axis=2)
        decay = jnp.exp(jnp.where(causal, g[:, :, :, None, :] - g[:, :, None, :, :], -jnp.inf))
        A = jnp.einsum('bhtk,bhsk,bhtsk->bhts', qc, kc, decay)
        o = (jnp.einsum('bhtk,bhkv->bhtv', qc * jnp.exp(g), S)
             + jnp.einsum('bhts,bhsv->bhtv', A, vc))
        g_end = g[:, :, -1:, :]
        S = (jnp.exp(g_end[:, :, 0, :])[..., None] * S
             + jnp.einsum('bhsk,bhsv->bhkv', kc * jnp.exp(g_end - g), vc))
        return S, o

    S0 = jnp.zeros((B, H_B, DK_B, DV_B), jnp.float32)
    S, o = lax.scan(step, S0, (chunks(q), chunks(k), chunks(v.astype(jnp.float32)), chunks(logf)))
    return o.transpose(1, 0, 3, 2, 4).reshape(B, T, H_B, DV_B), S


def _hgrn_sample(q, k, v, logf, S0):
    def step(S, xs):
        qt, kt, vt, lt = xs
        S = jnp.exp(lt)[..., None] * S + kt[..., None] * vt[..., None, :]
        return S, jnp.einsum('bhk,bhkv->bhv', qt, S)

    tm = lambda a: a.swapaxes(0, 1)
    S, o = lax.scan(step, S0.astype(jnp.float32),
                    (tm(q), tm(k), tm(v.astype(jnp.float32)), tm(logf)))
    return o.swapaxes(0, 1), S


def _hgrn_post(o, gb, gnorm_w, dtype):
    B, T = o.shape[:2]
    y = _rmsnorm(o, gnorm_w) * jax.nn.silu(gb.astype(jnp.float32))
    return y.reshape(B, T, H_B * DV_B).astype(dtype)


def _merge(oa, ob, gate_a, gate_b, w_branch_a, w_branch_b, w_out):
    m = jax.nn.sigmoid(gate_a) * (oa @ w_branch_a) + jax.nn.sigmoid(gate_b) * (ob @ w_branch_b)
    return m @ w_out


def _peer(x, w_query, sub_keys, expert_u, expert_v):
    shp = x.shape
    xt = x.reshape(-1, D_MODEL)
    n = xt.shape[0]
    nb = -(-n // PEER_BLOCK)
    xt = jnp.pad(xt, ((0, nb * PEER_BLOCK - n), (0, 0))).reshape(nb, PEER_BLOCK, D_MODEL)

    def one_block(xb):
        q = (xb @ w_query).reshape(PEER_BLOCK, H_P, 2, DK_P_HALF)
        s = jnp.einsum('thcd,hcnd->thcn', q, sub_keys).astype(jnp.float32)
        s1, i1 = lax.top_k(s[:, :, 0], TOPK_P)
        s2, i2 = lax.top_k(s[:, :, 1], TOPK_P)
        comb = (s1[..., :, None] + s2[..., None, :]).reshape(PEER_BLOCK, H_P, TOPK_P * TOPK_P)
        cs, ci = lax.top_k(comb, TOPK_P)
        e1 = jnp.take_along_axis(i1, ci // TOPK_P, axis=-1)
        e2 = jnp.take_along_axis(i2, ci % TOPK_P, axis=-1)
        idx = e1 * N_KEYS + e2
        w = jax.nn.softmax(cs, axis=-1)
        u = expert_u[idx]
        a = jax.nn.gelu(jnp.einsum('td,thkd->thk', xb, u).astype(jnp.float32), approximate=False)
        v = expert_v[idx]
        return jnp.einsum('thk,thkd->td', (w * a).astype(xb.dtype), v)

    out = lax.map(one_block, xt).reshape(-1, D_MODEL)[:n]
    return out.reshape(shp)


def setup_inputs(seed: int = 0) -> dict:
    key = jax.random.key(seed)
    ks = jax.random.split(key, 24)

    def nrm(k, shape, scale):
        return jax.random.normal(k, shape, jnp.float32) * scale

    n_pages = PAST_LEN // PAGE_SIZE
    n_used = DEC_BATCH * n_pages
    n_phys = (n_used * POOL_NUM + POOL_DEN - 1) // POOL_DEN
    perm = jax.random.permutation(ks[5], n_phys)
    page_table = perm[:n_used].reshape(DEC_BATCH, n_pages).astype(jnp.int32)
    return {
        'x_prompt': nrm(ks[0], (BATCH, SEQ, D_MODEL), 1.0),
        'x_sample': nrm(ks[1], (DEC_BATCH, DEC_SEQ, D_MODEL), 1.0),
        'cache_k': nrm(ks[2], (DEPTH, n_phys, PAGE_SIZE, H_A, 2, DH_QK), 1.0),
        'cache_v': nrm(ks[3], (DEPTH, n_phys, PAGE_SIZE, H_A, DV_A), 1.0),
        'state_hgrn': nrm(ks[4], (DEPTH, DEC_BATCH, H_B, DK_B, DV_B), 0.3),
        'page_table': page_table,
        'norm1_w': 1.0 + nrm(ks[6], (DEPTH, D_MODEL), 0.02),
        'w_in': nrm(ks[7], (DEPTH, D_MODEL, N_IN), D_MODEL ** -0.5),
        'lambda_q1': nrm(ks[8], (DEPTH, DH_QK), 0.1),
        'lambda_k1': nrm(ks[9], (DEPTH, DH_QK), 0.1),
        'lambda_q2': nrm(ks[10], (DEPTH, DH_QK), 0.1),
        'lambda_k2': nrm(ks[11], (DEPTH, DH_QK), 0.1),
        'subln_w': 1.0 + nrm(ks[12], (DEPTH, DV_A), 0.02),
        'lb_param': nrm(ks[13], (DEPTH + 1, H_B * DK_B), 0.1),
        'gnorm_w': 1.0 + nrm(ks[14], (DEPTH, DV_B), 0.02),
        'w_branch_a': nrm(ks[15], (DEPTH, H_A * DV_A, D_MODEL), (H_A * DV_A) ** -0.5),
        'w_branch_b': nrm(ks[16], (DEPTH, H_B * DV_B, D_MODEL), (H_B * DV_B) ** -0.5),
        'w_out': nrm(ks[17], (DEPTH, D_MODEL, D_MODEL), D_MODEL ** -0.5),
        'norm2_w': 1.0 + nrm(ks[18], (DEPTH, D_MODEL), 0.02),
        'w_query': nrm(ks[19], (DEPTH, D_MODEL, H_P * DK_P), D_MODEL ** -0.5),
        'sub_keys': nrm(ks[20], (DEPTH, H_P, 2, N_KEYS, DK_P_HALF), DK_P_HALF ** -0.5),
        'expert_u': nrm(ks[21], (DEPTH, N_EXPERTS, D_MODEL), D_MODEL ** -0.5),
        'expert_v': nrm(ks[22], (DEPTH, N_EXPERTS, D_MODEL), 0.5),
        'norm_f_w': 1.0 + nrm(ks[23], (D_MODEL,), 0.02),
    }


def reference(x_prompt, x_sample, cache_k, cache_v, state_hgrn, page_table,
              norm1_w, w_in, lambda_q1, lambda_k1, lambda_q2, lambda_k2, subln_w,
              lb_param, gnorm_w, w_branch_a, w_branch_b, w_out, norm2_w,
              w_query, sub_keys, expert_u, expert_v, norm_f_w):
    n_dec, n_pages = page_table.shape
    past = n_pages * cache_k.shape[2]
    dec_len = x_sample.shape[1]
    lb_all = jnp.cumsum(jax.nn.softmax(lb_param.astype(jnp.float32), axis=0), axis=0)
    hp, hs = x_prompt, x_sample
    kp_l, vp_l, sp_l, ks_l, vs_l, ss_l = [], [], [], [], [], []
    for l in range(DEPTH):
        lam_init = 0.8 - 0.6 * math.exp(-0.3 * l)
        lam = _diff_lambda(lambda_q1[l], lambda_k1[l], lambda_q2[l], lambda_k2[l], lam_init)
        lb = lb_all[l].reshape(H_B, DK_B)

        qa, ka, va, qb, fb, ib, gb, ga, gbr = _split_in(_rmsnorm(hp, norm1_w[l]) @ w_in[l])
        oa = _diff_post(_diff_attn_prompt(qa, ka, va, lam), subln_w[l], lam_init)
        hq, hk, hlf = _hgrn_inputs(qb, fb, lb)
        ob, s_p = _hgrn_prompt(hq, hk, ib, hlf)
        ob = _hgrn_post(ob, gb, gnorm_w[l], hp.dtype)
        hp = hp + _merge(oa, ob, ga, gbr, w_branch_a[l], w_branch_b[l], w_out[l])
        hp = hp + _peer(_rmsnorm(hp, norm2_w[l]), w_query[l], sub_keys[l], expert_u[l], expert_v[l])
        kp_l.append(ka)
        vp_l.append(va)
        sp_l.append(s_p.astype(state_hgrn.dtype))

        qa_s, ka_s, va_s, qb_s, fb_s, ib_s, gb_s, ga_s, gbr_s = _split_in(_rmsnorm(hs, norm1_w[l]) @ w_in[l])
        k_past = cache_k[l, page_table].reshape(n_dec, past, H_A, 2, DH_QK)
        v_past = cache_v[l, page_table].reshape(n_dec, past, H_A, DV_A)
        k_all = jnp.concatenate([k_past, ka_s.astype(k_past.dtype)], axis=1)
        v_all = jnp.concatenate([v_past, va_s.astype(v_past.dtype)], axis=1)
        q_pos = past + jnp.arange(dec_len)
        k_pos = jnp.arange(past + dec_len)
        oa_s = _diff_post(_diff_attn_core(qa_s, k_all, v_all, q_pos, k_pos, lam), subln_w[l], lam_init)
        hq_s, hk_s, hlf_s = _hgrn_inputs(qb_s, fb_s, lb)
        ob_s, s_s = _hgrn_sample(hq_s, hk_s, ib_s, hlf_s, state_hgrn[l])
        ob_s = _hgrn_post(ob_s, gb_s, gnorm_w[l], hs.dtype)
        hs = hs + _merge(oa_s, ob_s, ga_s, gbr_s, w_branch_a[l], w_branch_b[l], w_out[l])
        hs = hs + _peer(_rmsnorm(hs, norm2_w[l]), w_query[l], sub_keys[l], expert_u[l], expert_v[l])
        ks_l.append(ka_s)
        vs_l.append(va_s)
        ss_l.append(s_s.astype(state_hgrn.dtype))

    y_prompt = _rmsnorm(hp, norm_f_w)
    y_sample = _rmsnorm(hs, norm_f_w)
    return (y_prompt, y_sample, jnp.stack(kp_l), jnp.stack(vp_l), jnp.stack(sp_l),
            jnp.stack(ks_l), jnp.stack(vs_l), jnp.stack(ss_l))
```

```python
import functools
import math

import jax
import jax.numpy as jnp
from jax import lax
from jax.experimental import pallas as pl
from jax.experimental.pallas import tpu as pltpu

F32 = jnp.float32
BF16 = jnp.bfloat16

EPS = 1e-6
NEG = -1e30
LANES = 128
SUBLANES = 8
VMEM_LIMIT = 56 * 1024 * 1024

H_A = 4
DH_QK = 64
DV_A = 128
H_B = 4
DK_B = 128
DV_B = 128
N_KEYS = 128
H_P = 8
DK_P_HALF = 64
TOPK_P = 16
D_MODEL = 1024
N_QK = H_A * 2 * DH_QK
N_HG = 4 * H_B * DK_B
N_GT = 2 * D_MODEL
N_IN = 3 * N_QK + N_HG + N_GT


def _cparams(sem):
    return pltpu.CompilerParams(dimension_semantics=sem, vmem_limit_bytes=VMEM_LIMIT)


def _rms(x, w):
    return x * lax.rsqrt(jnp.mean(x * x, axis=-1, keepdims=True) + EPS) * w


def _sigmoid(x):
    return 1.0 / (1.0 + jnp.exp(-x))


def _silu(x):
    return x * _sigmoid(x)


def _dot(a, b):
    return jnp.dot(a, b, preferred_element_type=F32)


def _dot_nt(a, b):
    return lax.dot_general(a, b, (((1,), (1,)), ((), ())), preferred_element_type=F32)


def _inproj_kernel(x_ref, g_ref, w_ref, qa_ref, ka_ref, va_ref, hg_ref, gt_ref):
    nb = _rms(x_ref[...], g_ref[...]).astype(BF16)
    cw = N_QK

    def mm(c):
        return _dot(nb, w_ref[:, c * cw:(c + 1) * cw])

    qa_ref[...] = (mm(0) * (DH_QK ** -0.5)).astype(BF16)
    ka_ref[...] = mm(1)
    va_ref[...] = mm(2)
    for c in range(N_HG // cw):
        hg_ref[:, c * cw:(c + 1) * cw] = mm(3 + c)
    for c in range(N_GT // cw):
        gt_ref[:, c * cw:(c + 1) * cw] = mm(3 + N_HG // cw + c)


def _in_proj(x, g, w, tm):
    m = x.shape[0]
    row = lambda n: pl.BlockSpec((tm, n), lambda i: (i, 0))
    full = lambda a: pl.BlockSpec(a.shape, lambda i: (0, 0))
    return pl.pallas_call(
        _inproj_kernel,
        grid=(m // tm,),
        in_specs=[row(D_MODEL), full(g), full(w)],
        out_specs=[row(N_QK), row(N_QK), row(N_QK), row(N_HG), row(N_GT)],
        out_shape=[jax.ShapeDtypeStruct((m, N_QK), BF16),
                   jax.ShapeDtypeStruct((m, N_QK), F32),
                   jax.ShapeDtypeStruct((m, N_QK), F32),
                   jax.ShapeDtypeStruct((m, N_HG), F32),
                   jax.ShapeDtypeStruct((m, N_GT), F32)],
        compiler_params=_cparams(("parallel",)),
        name="in_proj",
    )(x, g, w)


def _attn_kernel(lam_ref, q_ref, k_ref, v_ref, w_ref, o_ref,
                 m1, l1, a1, m2, l2, a2, *, post_scale):
    qi = pl.program_id(2)
    ki = pl.program_id(3)
    tq = q_ref.shape[0]

    @pl.when(ki == 0)
    def _():
        for m, l, a in ((m1, l1, a1), (m2, l2, a2)):
            m[...] = jnp.full(m.shape, NEG, F32)
            l[...] = jnp.zeros(l.shape, F32)
            a[...] = jnp.zeros(a.shape, F32)

    def step(masked):
        q = q_ref[...]
        lane = lax.broadcasted_iota(jnp.int32, q.shape, 1)
        k = k_ref[...].astype(BF16)
        v = v_ref[...].astype(BF16)
        if masked:
            row = lax.broadcasted_iota(jnp.int32, (tq, tq), 0)
            col = lax.broadcasted_iota(jnp.int32, (tq, tq), 1)
            keep = col <= row
        for c, (m, l, a) in enumerate(((m1, l1, a1), (m2, l2, a2))):
            qc = jnp.where((lane >= DH_QK) == bool(c), q, jnp.zeros_like(q))
            s = _dot_nt(qc, k)
            if masked:
                s = jnp.where(keep, s, NEG)
            m_prev = m[:, :1]
            m_new = jnp.maximum(m_prev, jnp.max(s, axis=-1, keepdims=True))
            alpha = jnp.exp(m_prev - m_new)
            p = jnp.exp(s - m_new)
            l[...] = jnp.broadcast_to(alpha * l[:, :1] + jnp.sum(p, axis=-1, keepdims=True), l.shape)
            a[...] = alpha * a[...] + _dot(p.astype(BF16), v)
            m[...] = jnp.broadcast_to(m_new, m.shape)

    @pl.when(ki < qi)
    def _():
        step(False)

    @pl.when(ki == qi)
    def _():
        step(True)
        lam = lam_ref[0]
        o = a1[...] / l1[:, :1] - lam * (a2[...] / l2[:, :1])
        o_ref[...] = _rms(o, w_ref[...]) * post_scale


def _attn_prompt(lam, qa, ka, va, subln_w, batch, seq, tq, post_scale):
    nq = seq // tq
    qmap = lambda b, h, qi, ki: (b * nq + qi, h)
    kmap = lambda b, h, qi, ki: (b * nq + jnp.minimum(ki, qi), h)
    blk = lambda f: pl.BlockSpec((tq, DV_A), f)
    return pl.pallas_call(
        functools.partial(_attn_kernel, post_scale=post_scale),
        grid=(batch, H_A, nq, nq),
        in_specs=[pl.BlockSpec(memory_space=pltpu.SMEM),
                  blk(qmap), blk(kmap), blk(kmap),
                  pl.BlockSpec((1, DV_A), lambda b, h, qi, ki: (0, 0))],
        out_specs=blk(qmap),
        out_shape=jax.ShapeDtypeStruct((batch * seq, H_A * DV_A), F32),
        scratch_shapes=[pltpu.VMEM((tq, LANES), F32), pltpu.VMEM((tq, LANES), F32),
                        pltpu.VMEM((tq, DV_A), F32)] * 2,
        compiler_params=_cparams(("parallel", "parallel", "parallel", "arbitrary")),
        name="attn_prompt",
    )(lam, qa, ka, va, subln_w)


def _decode_kernel(pt_ref, lam_ref, q_ref, kn_ref, vn_ref, w_ref, *refs, n_pages, post_scale):
    k_refs = refs[:n_pages]
    v_refs = refs[n_pages:2 * n_pages]
    o_ref = refs[2 * n_pages]
    s_all = refs[2 * n_pages + 1]
    del pt_ref
    nmap = 2 * H_A
    width = N_QK

    q = q_ref[0].astype(F32)
    rows = lax.broadcasted_iota(jnp.int32, (LANES, width), 0)
    lane = lax.broadcasted_iota(jnp.int32, (LANES, width), 1)
    lane_h = lane // (2 * DH_QK)
    lane_c = (lane // DH_QK) % 2
    qrows = jnp.where(rows == lane_c * H_A + lane_h, jnp.broadcast_to(q, (LANES, width)), 0.0)
    qmat = jnp.transpose(qrows).astype(BF16)

    kn = jnp.broadcast_to(kn_ref[0], (SUBLANES, width)).astype(BF16)
    s_new = _dot(kn, qmat)[:1]
    m = s_new
    for i in range(n_pages):
        s = _dot(k_refs[i][0].astype(BF16), qmat)
        s_all[i] = s
        m = jnp.maximum(m, jnp.max(s, axis=0, keepdims=True))

    p_new = jnp.exp(s_new - m)
    l = p_new
    o8 = jnp.zeros((SUBLANES, width), F32)
    for i in range(n_pages):
        p = jnp.exp(s_all[i] - m)
        l = l + jnp.sum(p, axis=0, keepdims=True)
        pt = jnp.transpose(p)[:nmap].astype(BF16)
        o8 = o8 + _dot(pt, v_refs[i][0].astype(BF16))

    r8 = lax.broadcasted_iota(jnp.int32, (SUBLANES, LANES), 0)
    c8 = lax.broadcasted_iota(jnp.int32, (SUBLANES, LANES), 1)
    diag = r8 == c8

    def col(x):
        return jnp.sum(jnp.where(diag, jnp.broadcast_to(x, (SUBLANES, LANES)), 0.0),
                       axis=-1, keepdims=True)

    vn = vn_ref[0].astype(BF16).astype(F32)
    pn = p_new.astype(BF16).astype(F32)
    o8 = (o8 + col(pn) * vn) / col(l)
    lam = lam_ref[0]
    w = w_ref[...]
    for h in range(H_A):
        sl = slice(h * DV_A, (h + 1) * DV_A)
        d = o8[h:h + 1, sl] - lam * o8[H_A + h:H_A + h + 1, sl]
        o_ref[0, :, sl] = _rms(d, w) * post_scale


def _attn_decode(page_table, lam, qa, ka, va, subln_w, cache_k, cache_v, post_scale):
    nb, n_pages = page_table.shape
    page = cache_k.shape[1]
    width = cache_k.shape[2]
    one = lambda: pl.BlockSpec((1, 1, width), lambda b, pt: (b, 0, 0))
    pspec = lambda i: pl.BlockSpec((1, page, width), lambda b, pt: (pt[b, i], 0, 0))
    grid_spec = pltpu.PrefetchScalarGridSpec(
        num_scalar_prefetch=1,
        grid=(nb,),
        in_specs=[pl.BlockSpec(memory_space=pltpu.SMEM), one(), one(), one(),
                  pl.BlockSpec((1, DV_A), lambda b, pt: (0, 0))]
                 + [pspec(i) for i in range(n_pages)] * 2,
        out_specs=one(),
        scratch_shapes=[pltpu.VMEM((n_pages, page, LANES), F32)],
    )
    r3 = lambda a: a.reshape(nb, 1, width)
    out = pl.pallas_call(
        functools.partial(_decode_kernel, n_pages=n_pages, post_scale=post_scale),
        grid_spec=grid_spec,
        out_shape=jax.ShapeDtypeStruct((nb, 1, width), F32),
        compiler_params=_cparams(("arbitrary",)),
        name="attn_decode",
    )(page_table, lam, r3(qa), r3(ka), r3(va), subln_w,
      *([cache_k] * n_pages), *([cache_v] * n_pages))
    return out.reshape(nb, width)


def _hgrn_gates(qb, fb, lb):
    q = _silu(qb) * (DK_B ** -0.5)
    f = lb + (1.0 - lb) * _sigmoid(fb)
    return q, 1.0 - f, jnp.log(f)


def _hgrn_post(o, gb, w):
    return _rms(o, w) * _silu(gb)


def _hgrn_prompt_kernel(qb_ref, fb_ref, ib_ref, gb_ref, lb_ref, w_ref, o_ref, s_ref, st_ref):
    ci = pl.program_id(2)
    c = qb_ref.shape[0]
    dk = DK_B

    @pl.when(ci == 0)
    def _():
        st_ref[...] = jnp.zeros(st_ref.shape, F32)

    q, k, lf = _hgrn_gates(qb_ref[...], fb_ref[...], lb_ref[0])
    v = ib_ref[...]
    vb = v.astype(BF16)

    row = lax.broadcasted_iota(jnp.int32, (c, c), 0)
    col = lax.broadcasted_iota(jnp.int32, (c, c), 1)
    tril = jnp.where(col <= row, 1.0, 0.0).astype(BF16)
    hi = lf.astype(BF16)
    r1 = lf - hi.astype(F32)
    mid = r1.astype(BF16)
    lo = (r1 - mid.astype(F32)).astype(BF16)
    g = _dot(tril, hi) + _dot(tril, mid) + _dot(tril, lo)

    st = st_ref[...]
    o = _dot_nt((q * jnp.exp(g)).astype(BF16), st.astype(BF16))

    a = jnp.zeros((c, c), F32)
    m = SUBLANES
    while 2 * m <= c:
        nb = c // (2 * m)
        g3 = g.reshape(nb, 2 * m, dk)
        d = g3 - g3[:, m - 1:m, :]
        rin = lax.broadcasted_iota(jnp.int32, (nb, 2 * m, dk), 1)
        qs = jnp.where(rin >= m, q.reshape(nb, 2 * m, dk) * jnp.exp(jnp.minimum(d, 0.0)), 0.0)
        ks = jnp.where(rin < m, k.reshape(nb, 2 * m, dk) * jnp.exp(jnp.minimum(-d, 0.0)), 0.0)
        al = _dot_nt(qs.reshape(c, dk).astype(BF16), ks.reshape(c, dk).astype(BF16))
        a = a + jnp.where(row // (2 * m) == col // (2 * m), al, 0.0)
        m *= 2

    nb = c // SUBLANES
    g8 = g.reshape(nb, SUBLANES, dk)
    q8 = q.reshape(nb, SUBLANES, dk)
    k8 = k.reshape(nb, SUBLANES, dk)
    rin = lax.broadcasted_iota(jnp.int32, (nb, SUBLANES, dk), 1)
    ones = jnp.ones((dk, c), BF16)
    for s in range(SUBLANES):
        d = g8 - g8[:, s:s + 1, :]
        p = jnp.where(rin >= s, q8 * k8[:, s:s + 1, :] * jnp.exp(jnp.minimum(d, 0.0)), 0.0)
        r = _dot(p.reshape(c, dk).astype(BF16), ones)
        a = a + jnp.where(col == (row // SUBLANES) * SUBLANES + s, r, 0.0)

    o = o + _dot(a.astype(BF16), vb)
    o_ref[...] = _hgrn_post(o, gb_ref[...], w_ref[...])

    g_end = g[c - 1:c, :]
    kd = (k * jnp.exp(g_end - g)).astype(BF16)
    st_new = st * jnp.exp(g_end) + _dot(jnp.transpose(v).astype(BF16), kd)
    st_ref[...] = st_new

    @pl.when(ci == pl.num_programs(2) - 1)
    def _():
        s_ref[0, 0] = jnp.transpose(st_new)


def _hgrn_prompt(hg, lb, gnorm_w, batch, seq, chunk):
    nc = seq // chunk
    cmap = lambda j: (lambda b, h, c: (b * nc + c, j * H_B + h))
    blk = lambda j: pl.BlockSpec((chunk, DK_B), cmap(j))
    return pl.pallas_call(
        _hgrn_prompt_kernel,
        grid=(batch, H_B, nc),
        in_specs=[blk(0), blk(1), blk(2), blk(3),
                  pl.BlockSpec((1, 1, DK_B), lambda b, h, c: (h, 0, 0)),
                  pl.BlockSpec((1, DV_B), lambda b, h, c: (0, 0))],
        out_specs=[pl.BlockSpec((chunk, DV_B), lambda b, h, c: (b * nc + c, h)),
                   pl.BlockSpec((1, 1, DK_B, DV_B), lambda b, h, c: (b, h, 0, 0))],
        out_shape=[jax.ShapeDtypeStruct((batch * seq, H_B * DV_B), F32),
                   jax.ShapeDtypeStruct((batch, H_B, DK_B, DV_B), F32)],
        scratch_shapes=[pltpu.VMEM((DV_B, DK_B), F32)],
        compiler_params=_cparams(("parallel", "parallel", "arbitrary")),
        name="hgrn_prompt",
    )(hg, hg, hg, hg, lb, gnorm_w)


def _hgrn_sample_kernel(hg_ref, lb_ref, w_ref, s0_ref, o_ref, s_ref):
    nb = s0_ref.shape[0]
    for b in range(nb):
        for h in range(H_B):
            sl = lambda j: slice((j * H_B + h) * DK_B, (j * H_B + h + 1) * DK_B)
            row = lambda j: hg_ref[b, :, sl(j)]
            q, k, lf = _hgrn_gates(row(0), row(1), lb_ref[h])
            v = row(2)
            stack = jnp.concatenate(
                [q, k, jnp.exp(lf), jnp.zeros((DK_B - 3, DK_B), F32)], axis=0)
            cols = jnp.transpose(stack)
            s_new = cols[:, 2:3] * s0_ref[b, h] + cols[:, 1:2] * v
            s_ref[b, h] = s_new
            o = jnp.sum(cols[:, 0:1] * s_new, axis=0, keepdims=True)
            o_ref[b, :, h * DV_B:(h + 1) * DV_B] = _hgrn_post(o, row(3), w_ref[...])


def _hgrn_sample(hg, lb, gnorm_w, state, tb):
    nb = state.shape[0]
    return pl.pallas_call(
        _hgrn_sample_kernel,
        grid=(nb // tb,),
        in_specs=[pl.BlockSpec((tb, 1, N_HG), lambda i: (i, 0, 0)),
                  pl.BlockSpec((H_B, 1, DK_B), lambda i: (0, 0, 0)),
                  pl.BlockSpec((1, DV_B), lambda i: (0, 0)),
                  pl.BlockSpec((tb, H_B, DK_B, DV_B), lambda i: (i, 0, 0, 0))],
        out_specs=[pl.BlockSpec((tb, 1, H_B * DV_B), lambda i: (i, 0, 0)),
                   pl.BlockSpec((tb, H_B, DK_B, DV_B), lambda i: (i, 0, 0, 0))],
        out_shape=[jax.ShapeDtypeStruct((nb, 1, H_B * DV_B), F32),
                   jax.ShapeDtypeStruct(state.shape, F32)],
        compiler_params=_cparams(("parallel",)),
        name="hgrn_sample",
    )(hg.reshape(nb, 1, N_HG), lb, gnorm_w, state)


def _merge_kernel(x_ref, oa_ref, ob_ref, gt_ref, wa_ref, wb_ref, wo_ref, n2w_ref, wq_ref,
                  keys_ref, h_ref, nt_ref, s1_ref, s2_ref):
    ga = gt_ref[:, :D_MODEL]
    gb = gt_ref[:, D_MODEL:]
    m = (_sigmoid(ga) * _dot(oa_ref[...].astype(BF16), wa_ref[...])
         + _sigmoid(gb) * _dot(ob_ref[...].astype(BF16), wb_ref[...]))
    h = x_ref[...] + _dot(m.astype(BF16), wo_ref[...])
    h_ref[...] = h
    n2 = _rms(h, n2w_ref[...])
    nt_ref[...] = jnp.transpose(n2).astype(BF16)
    qp = _dot(n2.astype(BF16), wq_ref[...]).astype(BF16)
    for hh in range(H_P):
        qh = qp[:, hh * LANES:(hh + 1) * LANES]
        s1_ref[hh] = _dot_nt(keys_ref[hh, 0], qh)
        s2_ref[hh] = _dot_nt(keys_ref[hh, 1], qh)


def _merge(x, oa, ob, gt, wa, wb, wo, n2w, wq, keys, tm):
    t = x.shape[0]
    row = lambda n: pl.BlockSpec((tm, n), lambda i: (i, 0))
    full = lambda a: pl.BlockSpec(a.shape, lambda i: (0,) * a.ndim)
    sspec = pl.BlockSpec((H_P, N_KEYS, tm), lambda i: (0, 0, i))
    return pl.pallas_call(
        _merge_kernel,
        grid=(t // tm,),
        in_specs=[row(D_MODEL), row(N_QK), row(H_B * DV_B), row(N_GT),
                  full(wa), full(wb), full(wo), full(n2w), full(wq), full(keys)],
        out_specs=[row(D_MODEL), pl.BlockSpec((D_MODEL, tm), lambda i: (0, i)), sspec, sspec],
        out_shape=[jax.ShapeDtypeStruct((t, D_MODEL), F32),
                   jax.ShapeDtypeStruct((D_MODEL, t), BF16),
                   jax.ShapeDtypeStruct((H_P, N_KEYS, t), F32),
                   jax.ShapeDtypeStruct((H_P, N_KEYS, t), F32)],
        compiler_params=_cparams(("parallel",)),
        name="merge",
    )(x, oa, ob, gt, wa, wb, wo, n2w, wq, keys)


def _top_rows(s, buf):
    for p in range(TOPK_P):
        m = jnp.max(s, axis=0, keepdims=True)
        buf[p:p + 1, :] = m
        s = jnp.where(s == m, NEG, s)


def _topk_kernel(s1_ref, s2_ref, tau_ref, c_ref, t1_ref, t2_ref):
    tl = s1_ref.shape[2]
    r16 = lax.broadcasted_iota(jnp.int32, (TOPK_P, tl), 0)
    r8 = lax.broadcasted_iota(jnp.int32, (SUBLANES, tl), 0)
    for h in range(H_P):
        _top_rows(s1_ref[h], t1_ref)
        _top_rows(s2_ref[h], t2_ref)
        t1 = t1_ref[...]
        t2 = t2_ref[...]
        cand = [
            t1[0:1] + t2,
            t1[1:2] + t2[:8],
            jnp.where(r8 < 5, t1[2:3] + t2[:8], NEG),
            jnp.where(r8 < 4, t1[3:4] + t2[:8], NEG),
            jnp.where(r16 >= 4, t2[0:1] + t1, NEG),
            jnp.where(r8 >= 4, t2[1:2] + t1[:8], NEG),
            jnp.where(r8 == 4, t2[2:3] + t1[:8], NEG),
        ]
        cm = jnp.concatenate(cand, axis=0)
        cmax = t1[0:1] + t2[0:1]
        z = jnp.zeros((1, tl), F32)
        m = cmax
        for _ in range(TOPK_P):
            m = jnp.max(cm, axis=0, keepdims=True)
            z = z + jnp.exp(m - cmax)
            cm = jnp.where(cm == m, NEG, cm)
        tau_ref[h:h + 1, :] = m
        c_ref[h:h + 1, :] = cmax + jnp.log(z)


def _topk(s1, s2, tl):
    t = s1.shape[2]
    sspec = pl.BlockSpec((H_P, N_KEYS, tl), lambda i: (0, 0, i))
    ospec = pl.BlockSpec((H_P, tl), lambda i: (0, i))
    return pl.pallas_call(
        _topk_kernel,
        grid=(t // tl,),
        in_specs=[sspec, sspec],
        out_specs=[ospec, ospec],
        out_shape=[jax.ShapeDtypeStruct((H_P, t), F32)] * 2,
        scratch_shapes=[pltpu.VMEM((TOPK_P, tl), F32)] * 2,
        compiler_params=_cparams(("parallel",)),
        name="topk",
    )(s1, s2)


def _gelu(x):
    return 0.5 * x * (1.0 + lax.erf(x * (2.0 ** -0.5)))


def _peer_kernel(xt_ref, u_ref, vt_ref, s1_ref, s2_ref, tau_ref, c_ref, h_ref, nfw_ref,
                 y_ref, acc_ref, *, e1_per_step):
    j = pl.program_id(1)

    @pl.when(j == 0)
    def _():
        acc_ref[...] = jnp.zeros(acc_ref.shape, F32)

    ws = []
    for a in range(e1_per_step):
        e1 = j * e1_per_step + a
        w = jnp.zeros(s2_ref.shape[1:], F32)
        for hh in range(H_P):
            z = s2_ref[hh] + s1_ref[hh, pl.ds(e1, 1), :]
            w = w + jnp.where(z >= tau_ref[hh:hh + 1, :],
                              jnp.exp(z - c_ref[hh:hh + 1, :]), 0.0)
        ws.append(w)
    w = jnp.concatenate(ws, axis=0)
    act = _gelu(_dot(u_ref[...], xt_ref[...]))
    acc_ref[...] += _dot(vt_ref[...], (w * act).astype(BF16))

    @pl.when(j == pl.num_programs(1) - 1)
    def _():
        h = h_ref[...] + jnp.transpose(acc_ref[...])
        y_ref[...] = _rms(h, nfw_ref[...])


def _peer(xt, u, vt, s1, s2, tau, c, h, nfw, tm, e1_per_step):
    t = h.shape[0]
    te = e1_per_step * N_KEYS
    sspec = pl.BlockSpec((H_P, N_KEYS, tm), lambda i, j: (0, 0, i))
    tspec = pl.BlockSpec((H_P, tm), lambda i, j: (0, i))
    row = pl.BlockSpec((tm, D_MODEL), lambda i, j: (i, 0))
    return pl.pallas_call(
        functools.partial(_peer_kernel, e1_per_step=e1_per_step),
        grid=(t // tm, N_KEYS // e1_per_step),
        in_specs=[pl.BlockSpec((D_MODEL, tm), lambda i, j: (0, i)),
                  pl.BlockSpec((te, D_MODEL), lambda i, j: (j, 0)),
                  pl.BlockSpec((D_MODEL, te), lambda i, j: (0, j)),
                  sspec, sspec, tspec, tspec, row,
                  pl.BlockSpec((1, D_MODEL), lambda i, j: (0, 0))],
        out_specs=row,
        out_shape=jax.ShapeDtypeStruct((t, D_MODEL), F32),
        scratch_shapes=[pltpu.VMEM((D_MODEL, tm), F32)],
        compiler_params=_cparams(("parallel", "arbitrary")),
        name="peer",
    )(xt, u, vt, s1, s2, tau, c, h, nfw)


def _tile(n, pref):
    t = min(n, pref)
    assert n % t == 0, (n, t)
    return t


def kernel(x_prompt, x_sample, cache_k, cache_v, state_hgrn, page_table, norm1_w, w_in,
           lambda_q1, lambda_k1, lambda_q2, lambda_k2, subln_w, lb_param, gnorm_w,
           w_branch_a, w_branch_b, w_out, norm2_w, w_query, sub_keys, expert_u, expert_v,
           norm_f_w):
    depth = w_in.shape[0]
    assert depth == 1 and w_in.shape[2] == N_IN
    batch, seq, _ = x_prompt.shape
    n_dec, dec_len, _ = x_sample.shape
    assert dec_len == 1
    l = 0

    lam_init = 0.8 - 0.6 * math.exp(-0.3 * l)
    post_scale = 1.0 - lam_init
    f = F32
    lam = (jnp.exp(jnp.sum(lambda_q1[l].astype(f) * lambda_k1[l].astype(f)))
           - jnp.exp(jnp.sum(lambda_q2[l].astype(f) * lambda_k2[l].astype(f)))
           + lam_init).reshape(1)
    lb = jnp.cumsum(jax.nn.softmax(lb_param.astype(f), axis=0), axis=0)[l].reshape(H_B, 1, DK_B)

    n1w = norm1_w[l].reshape(1, D_MODEL)
    n2w = norm2_w[l].reshape(1, D_MODEL)
    nfw = norm_f_w.reshape(1, D_MODEL)
    sub_w = subln_w[l].reshape(1, DV_A)
    gn_w = gnorm_w[l].reshape(1, DV_B)
    win = w_in[l].astype(BF16)
    wa = w_branch_a[l].astype(BF16)
    wb = w_branch_b[l].astype(BF16)
    wo = w_out[l].astype(BF16)
    wq = w_query[l].astype(BF16)
    sk = sub_keys[l].astype(BF16)
    zk = jnp.zeros_like(sk)
    keys = jnp.stack([jnp.concatenate([sk[:, 0], zk[:, 0]], axis=-1),
                      jnp.concatenate([zk[:, 1], sk[:, 1]], axis=-1)], axis=1)
    u = expert_u[l].astype(BF16)
    vt = jnp.transpose(expert_v[l]).astype(BF16)

    def tail(x2, oa, ob, gt, tm_merge, tl, tm_peer):
        h, nt, s1, s2 = _merge(x2, oa, ob, gt, wa, wb, wo, n2w, wq, keys, tm_merge)
        tau, c = _topk(s1, s2, tl)
        return _peer(nt, u, vt, s1, s2, tau, c, h, nfw, tm_peer, 2)

    t = batch * seq
    xp = x_prompt.reshape(t, D_MODEL)
    qa, ka, va, hg, gt = _in_proj(xp, n1w, win, _tile(t, 256))
    oa = _attn_prompt(lam, qa, ka, va, sub_w, batch, seq, _tile(seq, 512), post_scale)
    ob, s_p = _hgrn_prompt(hg, lb, gn_w, batch, seq, _tile(seq, 128))
    y_p = tail(xp, oa, ob, gt, _tile(t, 256), _tile(t, 256), _tile(t, 512))

    xs = x_sample.reshape(n_dec, D_MODEL)
    qa_s, ka_s, va_s, hg_s, gt_s = _in_proj(xs, n1w, win, _tile(n_dec, 128))
    n_phys, page = cache_k.shape[1], cache_k.shape[2]
    oa_s = _attn_decode(page_table, lam, qa_s, ka_s, va_s, sub_w,
                        cache_k[l].reshape(n_phys, page, N_QK),
                        cache_v[l].reshape(n_phys, page, N_QK), post_scale)
    ob_s, s_s = _hgrn_sample(hg_s, lb, gn_w, state_hgrn[l], _tile(n_dec, 8))
    y_s = tail(xs, oa_s, ob_s.reshape(n_dec, H_B * DV_B), gt_s,
               _tile(n_dec, 128), _tile(n_dec, 128), _tile(n_dec, 128))

    return (y_p.reshape(batch, seq, D_MODEL),
            y_s.reshape(n_dec, 1, D_MODEL),
            ka.reshape(1, batch, seq, H_A, 2, DH_QK),
            va.reshape(1, batch, seq, H_A, DV_A),
            s_p.reshape(1, batch, H_B, DK_B, DV_B),
            ka_s.reshape(1, n_dec, 1, H_A, 2, DH_QK),
            va_s.reshape(1, n_dec, 1, H_A, DV_A),
            s_s.reshape(1, n_dec, H_B, DK_B, DV_B))
```

```python
import functools
import math

import jax
import jax.numpy as jnp
from jax import lax
from jax.experimental import pallas as pl
from jax.experimental.pallas import tpu as pltpu

F32 = jnp.float32
BF16 = jnp.bfloat16

EPS = 1e-6
NEG = -1e30
LOG2E = math.log2(math.e)
LANES = 128
SUBLANES = 8
VMEM_LIMIT = 56 * 1024 * 1024

H_A = 4
DH_QK = 64
DV_A = 128
H_B = 4
DK_B = 128
DV_B = 128
N_KEYS = 128
H_P = 8
DK_P_HALF = 64
TOPK_P = 16
D_MODEL = 1024
N_QK = H_A * 2 * DH_QK
N_HG = 4 * H_B * DK_B
N_GT = 2 * D_MODEL
N_IN = 3 * N_QK + N_HG + N_GT


def _cparams(sem):
    return pltpu.CompilerParams(dimension_semantics=sem, vmem_limit_bytes=VMEM_LIMIT)


def _rms(x, w):
    return x * lax.rsqrt(jnp.mean(x * x, axis=-1, keepdims=True) + EPS) * w


def _sigmoid(x):
    return 1.0 / (1.0 + jnp.exp(-x))


def _silu(x):
    return x * _sigmoid(x)


def _dot(a, b):
    return jnp.dot(a, b, preferred_element_type=F32)


def _dot_nt(a, b):
    return lax.dot_general(a, b, (((1,), (1,)), ((), ())), preferred_element_type=F32)


def _inproj_kernel(x_ref, g_ref, w_ref, qa_ref, ka_ref, va_ref, hg_ref, gt_ref):
    nb = _rms(x_ref[...], g_ref[...]).astype(BF16)
    cw = N_QK

    def mm(c):
        return _dot(nb, w_ref[:, c * cw:(c + 1) * cw])

    qa_ref[...] = (mm(0) * (DH_QK ** -0.5 * LOG2E)).astype(BF16)
    ka_ref[...] = mm(1)
    va_ref[...] = mm(2)
    for c in range(N_HG // cw):
        hg_ref[:, c * cw:(c + 1) * cw] = mm(3 + c)
    for c in range(N_GT // cw):
        gt_ref[:, c * cw:(c + 1) * cw] = mm(3 + N_HG // cw + c)


def _in_proj(x, g, w, tm):
    m = x.shape[0]
    row = lambda n: pl.BlockSpec((tm, n), lambda i: (i, 0))
    full = lambda a: pl.BlockSpec(a.shape, lambda i: (0, 0))
    return pl.pallas_call(
        _inproj_kernel,
        grid=(m // tm,),
        in_specs=[row(D_MODEL), full(g), full(w)],
        out_specs=[row(N_QK), row(N_QK), row(N_QK), row(N_HG), row(N_GT)],
        out_shape=[jax.ShapeDtypeStruct((m, N_QK), BF16),
                   jax.ShapeDtypeStruct((m, N_QK), F32),
                   jax.ShapeDtypeStruct((m, N_QK), F32),
                   jax.ShapeDtypeStruct((m, N_HG), F32),
                   jax.ShapeDtypeStruct((m, N_GT), F32)],
        compiler_params=_cparams(("parallel",)),
        name="in_proj",
    )(x, g, w)


def _attn_kernel(lam_ref, q_ref, k_ref, v_ref, w_ref, o_ref,
                 m1, l1, a1, m2, l2, a2, *, post_scale):
    qi = pl.program_id(2)
    ki = pl.program_id(3)
    tq = q_ref.shape[0]
    tk = k_ref.shape[0]

    @pl.when(ki == 0)
    def _():
        for m, l, a in ((m1, l1, a1), (m2, l2, a2)):
            m[...] = jnp.full(m.shape, NEG, F32)
            l[...] = jnp.zeros(l.shape, F32)
            a[...] = jnp.zeros(a.shape, F32)

    def step(masked):
        q = q_ref[...]
        lane = lax.broadcasted_iota(jnp.int32, q.shape, 1)
        k = k_ref[...].astype(BF16)
        vt = jnp.transpose(v_ref[...]).astype(BF16)
        if masked:
            krow = lax.broadcasted_iota(jnp.int32, (tk, tq), 0)
            qcol = lax.broadcasted_iota(jnp.int32, (tk, tq), 1)
            keep = krow <= qcol
        for c, (m, l, a) in enumerate(((m1, l1, a1), (m2, l2, a2))):
            qc = jnp.where((lane >= DH_QK) == bool(c), q, jnp.zeros_like(q))
            s = _dot_nt(k, qc)
            if masked:
                s = jnp.where(keep, s, NEG)
            m_prev = m[...]
            m_new = jnp.maximum(m_prev, jnp.max(s, axis=0, keepdims=True))
            alpha = jnp.exp2(m_prev - m_new)
            p = jnp.exp2(s - m_new)
            l[...] = alpha * l[...] + jnp.sum(p, axis=0, keepdims=True)
            a[...] = alpha * a[...] + _dot(vt, p.astype(BF16))
            m[...] = m_new

    @pl.when(ki < qi)
    def _():
        step(False)

    @pl.when(ki == qi)
    def _():
        step(True)
        lam = lam_ref[0]
        ot = a1[...] / l1[...] - lam * (a2[...] / l2[...])
        o_ref[...] = _rms(jnp.transpose(ot), w_ref[...]) * post_scale


def _attn_prompt(lam, qa, ka, va, subln_w, batch, seq, tq, post_scale):
    nq = seq // tq
    qmap = lambda b, h, qi, ki: (b * nq + qi, h)
    kmap = lambda b, h, qi, ki: (b * nq + jnp.minimum(ki, qi), h)
    blk = lambda f: pl.BlockSpec((tq, DV_A), f)
    return pl.pallas_call(
        functools.partial(_attn_kernel, post_scale=post_scale),
        grid=(batch, H_A, nq, nq),
        in_specs=[pl.BlockSpec(memory_space=pltpu.SMEM),
                  blk(qmap), blk(kmap), blk(kmap),
                  pl.BlockSpec((1, DV_A), lambda b, h, qi, ki: (0, 0))],
        out_specs=blk(qmap),
        out_shape=jax.ShapeDtypeStruct((batch * seq, H_A * DV_A), F32),
        scratch_shapes=[pltpu.VMEM((1, tq), F32), pltpu.VMEM((1, tq), F32),
                        pltpu.VMEM((DV_A, tq), F32)] * 2,
        compiler_params=_cparams(("parallel", "parallel", "parallel", "arbitrary")),
        name="attn_prompt",
    )(lam, qa, ka, va, subln_w)


def _decode_kernel(pt_ref, lam_ref, q_ref, kn_ref, vn_ref, w_ref, *refs, n_pages, post_scale):
    k_refs = refs[:n_pages]
    v_refs = refs[n_pages:2 * n_pages]
    o_ref = refs[2 * n_pages]
    s_all = refs[2 * n_pages + 1]
    del pt_ref
    nmap = 2 * H_A
    width = N_QK

    q = q_ref[0].astype(F32)
    rows = lax.broadcasted_iota(jnp.int32, (LANES, width), 0)
    lane = lax.broadcasted_iota(jnp.int32, (LANES, width), 1)
    lane_h = lane // (2 * DH_QK)
    lane_c = (lane // DH_QK) % 2
    qrows = jnp.where(rows == lane_c * H_A + lane_h, jnp.broadcast_to(q, (LANES, width)), 0.0)
    qmat = jnp.transpose(qrows).astype(BF16)

    kn = jnp.broadcast_to(kn_ref[0], (SUBLANES, width)).astype(BF16)
    s_new = _dot(kn, qmat)[:1]
    m = s_new
    for i in range(n_pages):
        s = _dot(k_refs[i][0].astype(BF16), qmat)
        s_all[i] = s
        m = jnp.maximum(m, jnp.max(s, axis=0, keepdims=True))

    p_new = jnp.exp2(s_new - m)
    l = p_new
    o8 = jnp.zeros((SUBLANES, width), F32)
    for i in range(n_pages):
        p = jnp.exp2(s_all[i] - m)
        l = l + jnp.sum(p, axis=0, keepdims=True)
        pt = jnp.transpose(p)[:nmap].astype(BF16)
        o8 = o8 + _dot(pt, v_refs[i][0].astype(BF16))

    r8 = lax.broadcasted_iota(jnp.int32, (SUBLANES, LANES), 0)
    c8 = lax.broadcasted_iota(jnp.int32, (SUBLANES, LANES), 1)
    diag = r8 == c8

    def col(x):
        return jnp.sum(jnp.where(diag, jnp.broadcast_to(x, (SUBLANES, LANES)), 0.0),
                       axis=-1, keepdims=True)

    vn = vn_ref[0].astype(BF16).astype(F32)
    pn = p_new.astype(BF16).astype(F32)
    o8 = (o8 + col(pn) * vn) / col(l)
    lam = lam_ref[0]
    w = w_ref[...]
    for h in range(H_A):
        sl = slice(h * DV_A, (h + 1) * DV_A)
        d = o8[h:h + 1, sl] - lam * o8[H_A + h:H_A + h + 1, sl]
        o_ref[0, :, sl] = _rms(d, w) * post_scale


def _attn_decode(page_table, lam, qa, ka, va, subln_w, cache_k, cache_v, post_scale):
    nb, n_pages = page_table.shape
    page = cache_k.shape[1]
    width = cache_k.shape[2]
    one = lambda: pl.BlockSpec((1, 1, width), lambda b, pt: (b, 0, 0))
    pspec = lambda i: pl.BlockSpec((1, page, width), lambda b, pt: (pt[b, i], 0, 0))
    grid_spec = pltpu.PrefetchScalarGridSpec(
        num_scalar_prefetch=1,
        grid=(nb,),
        in_specs=[pl.BlockSpec(memory_space=pltpu.SMEM), one(), one(), one(),
                  pl.BlockSpec((1, DV_A), lambda b, pt: (0, 0))]
                 + [pspec(i) for i in range(n_pages)] * 2,
        out_specs=one(),
        scratch_shapes=[pltpu.VMEM((n_pages, page, LANES), F32)],
    )
    r3 = lambda a: a.reshape(nb, 1, width)
    out = pl.pallas_call(
        functools.partial(_decode_kernel, n_pages=n_pages, post_scale=post_scale),
        grid_spec=grid_spec,
        out_shape=jax.ShapeDtypeStruct((nb, 1, width), F32),
        compiler_params=_cparams(("arbitrary",)),
        name="attn_decode",
    )(page_table, lam, r3(qa), r3(ka), r3(va), subln_w,
      *([cache_k] * n_pages), *([cache_v] * n_pages))
    return out.reshape(nb, width)


def _hgrn_gates(qb, fb, lb):
    q = _silu(qb) * (DK_B ** -0.5)
    f = lb + (1.0 - lb) * _sigmoid(fb)
    return q, 1.0 - f, jnp.log(f)


def _hgrn_post(o, gb, w):
    return _rms(o, w) * _silu(gb)


def _hgrn_prompt_kernel(qb_ref, fb_ref, ib_ref, gb_ref, lb_ref, w_ref, o_ref, s_ref, st_ref):
    ci = pl.program_id(2)
    c = qb_ref.shape[0]
    dk = DK_B

    @pl.when(ci == 0)
    def _():
        st_ref[...] = jnp.zeros(st_ref.shape, F32)

    q, k, lf = _hgrn_gates(qb_ref[...], fb_ref[...], lb_ref[0])
    v = ib_ref[...]
    vb = v.astype(BF16)

    row = lax.broadcasted_iota(jnp.int32, (c, c), 0)
    col = lax.broadcasted_iota(jnp.int32, (c, c), 1)
    tril = jnp.where(col <= row, 1.0, 0.0).astype(BF16)
    hi = lf.astype(BF16)
    r1 = lf - hi.astype(F32)
    mid = r1.astype(BF16)
    lo = (r1 - mid.astype(F32)).astype(BF16)
    g = _dot(tril, hi) + _dot(tril, mid) + _dot(tril, lo)

    st = st_ref[...]
    o = _dot_nt((q * jnp.exp(g)).astype(BF16), st.astype(BF16))

    a = jnp.zeros((c, c), F32)
    m = SUBLANES
    while 2 * m <= c:
        nb = c // (2 * m)
        g3 = g.reshape(nb, 2 * m, dk)
        d = g3 - g3[:, m - 1:m, :]
        rin = lax.broadcasted_iota(jnp.int32, (nb, 2 * m, dk), 1)
        qs = jnp.where(rin >= m, q.reshape(nb, 2 * m, dk) * jnp.exp(jnp.minimum(d, 0.0)), 0.0)
        ks = jnp.where(rin < m, k.reshape(nb, 2 * m, dk) * jnp.exp(jnp.minimum(-d, 0.0)), 0.0)
        al = _dot_nt(qs.reshape(c, dk).astype(BF16), ks.reshape(c, dk).astype(BF16))
        a = a + jnp.where(row // (2 * m) == col // (2 * m), al, 0.0)
        m *= 2

    nb = c // SUBLANES
    g8 = g.reshape(nb, SUBLANES, dk)
    q8 = q.reshape(nb, SUBLANES, dk)
    k8 = k.reshape(nb, SUBLANES, dk)
    rin = lax.broadcasted_iota(jnp.int32, (nb, SUBLANES, dk), 1)
    ones = jnp.ones((dk, c), BF16)
    for s in range(SUBLANES):
        d = g8 - g8[:, s:s + 1, :]
        p = jnp.where(rin >= s, q8 * k8[:, s:s + 1, :] * jnp.exp(jnp.minimum(d, 0.0)), 0.0)
        r = _dot(p.reshape(c, dk).astype(BF16), ones)
        a = a + jnp.where(col == (row // SUBLANES) * SUBLANES + s, r, 0.0)

    o = o + _dot(a.astype(BF16), vb)
    o_ref[...] = _hgrn_post(o, gb_ref[...], w_ref[...])

    g_end = g[c - 1:c, :]
    kd = (k * jnp.exp(g_end - g)).astype(BF16)
    st_new = st * jnp.exp(g_end) + _dot(jnp.transpose(v).astype(BF16), kd)
    st_ref[...] = st_new

    @pl.when(ci == pl.num_programs(2) - 1)
    def _():
        s_ref[0, 0] = jnp.transpose(st_new)


def _hgrn_prompt(hg, lb, gnorm_w, batch, seq, chunk):
    nc = seq // chunk
    cmap = lambda j: (lambda b, h, c: (b * nc + c, j * H_B + h))
    blk = lambda j: pl.BlockSpec((chunk, DK_B), cmap(j))
    return pl.pallas_call(
        _hgrn_prompt_kernel,
        grid=(batch, H_B, nc),
        in_specs=[blk(0), blk(1), blk(2), blk(3),
                  pl.BlockSpec((1, 1, DK_B), lambda b, h, c: (h, 0, 0)),
                  pl.BlockSpec((1, DV_B), lambda b, h, c: (0, 0))],
        out_specs=[pl.BlockSpec((chunk, DV_B), lambda b, h, c: (b * nc + c, h)),
                   pl.BlockSpec((1, 1, DK_B, DV_B), lambda b, h, c: (b, h, 0, 0))],
        out_shape=[jax.ShapeDtypeStruct((batch * seq, H_B * DV_B), F32),
                   jax.ShapeDtypeStruct((batch, H_B, DK_B, DV_B), F32)],
        scratch_shapes=[pltpu.VMEM((DV_B, DK_B), F32)],
        compiler_params=_cparams(("parallel", "parallel", "arbitrary")),
        name="hgrn_prompt",
    )(hg, hg, hg, hg, lb, gnorm_w)


def _hgrn_sample_kernel(hg_ref, lb_ref, w_ref, s0_ref, o_ref, s_ref):
    nb = s0_ref.shape[0]
    for b in range(nb):
        for h in range(H_B):
            sl = lambda j: slice((j * H_B + h) * DK_B, (j * H_B + h + 1) * DK_B)
            row = lambda j: hg_ref[b, :, sl(j)]
            q, k, lf = _hgrn_gates(row(0), row(1), lb_ref[h])
            v = row(2)
            stack = jnp.concatenate(
                [q, k, jnp.exp(lf), jnp.zeros((DK_B - 3, DK_B), F32)], axis=0)
            cols = jnp.transpose(stack)
            s_new = cols[:, 2:3] * s0_ref[b, h] + cols[:, 1:2] * v
            s_ref[b, h] = s_new
            o = jnp.sum(cols[:, 0:1] * s_new, axis=0, keepdims=True)
            o_ref[b, :, h * DV_B:(h + 1) * DV_B] = _hgrn_post(o, row(3), w_ref[...])


def _hgrn_sample(hg, lb, gnorm_w, state, tb):
    nb = state.shape[0]
    return pl.pallas_call(
        _hgrn_sample_kernel,
        grid=(nb // tb,),
        in_specs=[pl.BlockSpec((tb, 1, N_HG), lambda i: (i, 0, 0)),
                  pl.BlockSpec((H_B, 1, DK_B), lambda i: (0, 0, 0)),
                  pl.BlockSpec((1, DV_B), lambda i: (0, 0)),
                  pl.BlockSpec((tb, H_B, DK_B, DV_B), lambda i: (i, 0, 0, 0))],
        out_specs=[pl.BlockSpec((tb, 1, H_B * DV_B), lambda i: (i, 0, 0)),
                   pl.BlockSpec((tb, H_B, DK_B, DV_B), lambda i: (i, 0, 0, 0))],
        out_shape=[jax.ShapeDtypeStruct((nb, 1, H_B * DV_B), F32),
                   jax.ShapeDtypeStruct(state.shape, F32)],
        compiler_params=_cparams(("parallel",)),
        name="hgrn_sample",
    )(hg.reshape(nb, 1, N_HG), lb, gnorm_w, state)


def _merge_kernel(x_ref, oa_ref, ob_ref, gt_ref, wa_ref, wb_ref, wo_ref, n2w_ref, wq_ref,
                  keys_ref, h_ref, nt_ref, s1_ref, s2_ref):
    ga = gt_ref[:, :D_MODEL]
    gb = gt_ref[:, D_MODEL:]
    m = (_sigmoid(ga) * _dot(oa_ref[...].astype(BF16), wa_ref[...])
         + _sigmoid(gb) * _dot(ob_ref[...].astype(BF16), wb_ref[...]))
    h = x_ref[...] + _dot(m.astype(BF16), wo_ref[...])
    h_ref[...] = h
    n2 = _rms(h, n2w_ref[...])
    nt_ref[...] = jnp.transpose(n2).astype(BF16)
    qp = _dot(n2.astype(BF16), wq_ref[...]).astype(BF16)
    for hh in range(H_P):
        qh = qp[:, hh * LANES:(hh + 1) * LANES]
        s1_ref[hh] = _dot_nt(keys_ref[hh, 0], qh)
        s2_ref[hh] = _dot_nt(keys_ref[hh, 1], qh)


def _merge(x, oa, ob, gt, wa, wb, wo, n2w, wq, keys, tm):
    t = x.shape[0]
    row = lambda n: pl.BlockSpec((tm, n), lambda i: (i, 0))
    full = lambda a: pl.BlockSpec(a.shape, lambda i: (0,) * a.ndim)
    sspec = pl.BlockSpec((H_P, N_KEYS, tm), lambda i: (0, 0, i))
    return pl.pallas_call(
        _merge_kernel,
        grid=(t // tm,),
        in_specs=[row(D_MODEL), row(N_QK), row(H_B * DV_B), row(N_GT),
                  full(wa), full(wb), full(wo), full(n2w), full(wq), full(keys)],
        out_specs=[row(D_MODEL), pl.BlockSpec((D_MODEL, tm), lambda i: (0, i)), sspec, sspec],
        out_shape=[jax.ShapeDtypeStruct((t, D_MODEL), F32),
                   jax.ShapeDtypeStruct((D_MODEL, t), BF16),
                   jax.ShapeDtypeStruct((H_P, N_KEYS, t), F32),
                   jax.ShapeDtypeStruct((H_P, N_KEYS, t), F32)],
        compiler_params=_cparams(("parallel",)),
        name="merge",
    )(x, oa, ob, gt, wa, wb, wo, n2w, wq, keys)


def _top_rows(s, buf):
    for p in range(TOPK_P):
        m = jnp.max(s, axis=0, keepdims=True)
        buf[p:p + 1, :] = m
        s = jnp.where(s == m, NEG, s)


def _topk_kernel(s1_ref, s2_ref, tau_ref, c_ref, t1_ref, t2_ref):
    tl = s1_ref.shape[2]
    r16 = lax.broadcasted_iota(jnp.int32, (TOPK_P, tl), 0)
    r8 = lax.broadcasted_iota(jnp.int32, (SUBLANES, tl), 0)
    for h in range(H_P):
        _top_rows(s1_ref[h], t1_ref)
        _top_rows(s2_ref[h], t2_ref)
        t1 = t1_ref[...]
        t2 = t2_ref[...]
        cand = [
            t1[0:1] + t2,
            t1[1:2] + t2[:8],
            jnp.where(r8 < 5, t1[2:3] + t2[:8], NEG),
            jnp.where(r8 < 4, t1[3:4] + t2[:8], NEG),
            jnp.where(r16 >= 4, t2[0:1] + t1, NEG),
            jnp.where(r8 >= 4, t2[1:2] + t1[:8], NEG),
            jnp.where(r8 == 4, t2[2:3] + t1[:8], NEG),
        ]
        cm = jnp.concatenate(cand, axis=0)
        cmax = t1[0:1] + t2[0:1]
        z = jnp.zeros((1, tl), F32)
        m = cmax
        for _ in range(TOPK_P):
            m = jnp.max(cm, axis=0, keepdims=True)
            z = z + jnp.exp(m - cmax)
            cm = jnp.where(cm == m, NEG, cm)
        tau_ref[h:h + 1, :] = m
        c_ref[h:h + 1, :] = cmax + jnp.log(z)


def _topk(s1, s2, tl):
    t = s1.shape[2]
    sspec = pl.BlockSpec((H_P, N_KEYS, tl), lambda i: (0, 0, i))
    ospec = pl.BlockSpec((H_P, tl), lambda i: (0, i))
    return pl.pallas_call(
        _topk_kernel,
        grid=(t // tl,),
        in_specs=[sspec, sspec],
        out_specs=[ospec, ospec],
        out_shape=[jax.ShapeDtypeStruct((H_P, t), F32)] * 2,
        scratch_shapes=[pltpu.VMEM((TOPK_P, tl), F32)] * 2,
        compiler_params=_cparams(("parallel",)),
        name="topk",
    )(s1, s2)


def _gelu(x):
    return 0.5 * x * (1.0 + lax.erf(x * (2.0 ** -0.5)))


def _peer_kernel(xt_ref, u_ref, v_ref, s1_ref, s2_ref, tau_ref, c_ref, h_ref, nfw_ref,
                 y_ref, acc_ref, *, e1_per_step):
    j = pl.program_id(1)

    @pl.when(j == 0)
    def _():
        acc_ref[...] = jnp.zeros(acc_ref.shape, F32)

    ws = []
    for a in range(e1_per_step):
        e1 = j * e1_per_step + a
        w = jnp.zeros(s2_ref.shape[1:], F32)
        for hh in range(H_P):
            z = s2_ref[hh] + s1_ref[hh, pl.ds(e1, 1), :]
            w = w + jnp.where(z >= tau_ref[hh:hh + 1, :],
                              jnp.exp(z - c_ref[hh:hh + 1, :]), 0.0)
        ws.append(w)
    w = jnp.concatenate(ws, axis=0)
    act = _gelu(_dot(u_ref[...], xt_ref[...]))
    acc_ref[...] += _dot(jnp.transpose(w * act).astype(BF16), v_ref[...])

    @pl.when(j == pl.num_programs(1) - 1)
    def _():
        y_ref[...] = _rms(h_ref[...] + acc_ref[...], nfw_ref[...])


def _peer(xt, u, v, s1, s2, tau, c, h, nfw, tm, e1_per_step):
    t = h.shape[0]
    te = e1_per_step * N_KEYS
    sspec = pl.BlockSpec((H_P, N_KEYS, tm), lambda i, j: (0, 0, i))
    tspec = pl.BlockSpec((H_P, tm), lambda i, j: (0, i))
    row = pl.BlockSpec((tm, D_MODEL), lambda i, j: (i, 0))
    return pl.pallas_call(
        functools.partial(_peer_kernel, e1_per_step=e1_per_step),
        grid=(t // tm, N_KEYS // e1_per_step),
        in_specs=[pl.BlockSpec((D_MODEL, tm), lambda i, j: (0, i)),
                  pl.BlockSpec((te, D_MODEL), lambda i, j: (j, 0)),
                  pl.BlockSpec((te, D_MODEL), lambda i, j: (j, 0)),
                  sspec, sspec, tspec, tspec, row,
                  pl.BlockSpec((1, D_MODEL), lambda i, j: (0, 0))],
        out_specs=row,
        out_shape=jax.ShapeDtypeStruct((t, D_MODEL), F32),
        scratch_shapes=[pltpu.VMEM((tm, D_MODEL), F32)],
        compiler_params=_cparams(("parallel", "arbitrary")),
        name="peer",
    )(xt, u, v, s1, s2, tau, c, h, nfw)


def _tile(n, pref):
    t = min(n, pref)
    assert n % t == 0, (n, t)
    return t


def kernel(x_prompt, x_sample, cache_k, cache_v, state_hgrn, page_table, norm1_w, w_in,
           lambda_q1, lambda_k1, lambda_q2, lambda_k2, subln_w, lb_param, gnorm_w,
           w_branch_a, w_branch_b, w_out, norm2_w, w_query, sub_keys, expert_u, expert_v,
           norm_f_w):
    depth = w_in.shape[0]
    assert depth == 1 and w_in.shape[2] == N_IN
    batch, seq, _ = x_prompt.shape
    n_dec, dec_len, _ = x_sample.shape
    assert dec_len == 1
    l = 0

    lam_init = 0.8 - 0.6 * math.exp(-0.3 * l)
    post_scale = 1.0 - lam_init
    f = F32
    lam = (jnp.exp(jnp.sum(lambda_q1[l].astype(f) * lambda_k1[l].astype(f)))
           - jnp.exp(jnp.sum(lambda_q2[l].astype(f) * lambda_k2[l].astype(f)))
           + lam_init).reshape(1)
    lb = jnp.cumsum(jax.nn.softmax(lb_param.astype(f), axis=0), axis=0)[l].reshape(H_B, 1, DK_B)

    n1w = norm1_w[l].reshape(1, D_MODEL)
    n2w = norm2_w[l].reshape(1, D_MODEL)
    nfw = norm_f_w.reshape(1, D_MODEL)
    sub_w = subln_w[l].reshape(1, DV_A)
    gn_w = gnorm_w[l].reshape(1, DV_B)
    win = w_in[l].astype(BF16)
    wa = w_branch_a[l].astype(BF16)
    wb = w_branch_b[l].astype(BF16)
    wo = w_out[l].astype(BF16)
    wq = w_query[l].astype(BF16)
    sk = sub_keys[l].astype(BF16)
    zk = jnp.zeros_like(sk)
    keys = jnp.stack([jnp.concatenate([sk[:, 0], zk[:, 0]], axis=-1),
                      jnp.concatenate([zk[:, 1], sk[:, 1]], axis=-1)], axis=1)
    u = expert_u.reshape(expert_u.shape[1:]).astype(BF16)
    v = expert_v.reshape(expert_v.shape[1:]).astype(BF16)

    def tail(x2, oa, ob, gt, tm_merge, tl, tm_peer):
        h, nt, s1, s2 = _merge(x2, oa, ob, gt, wa, wb, wo, n2w, wq, keys, tm_merge)
        tau, c = _topk(s1, s2, tl)
        return _peer(nt, u, v, s1, s2, tau, c, h, nfw, tm_peer, 2)

    t = batch * seq
    xp = x_prompt.reshape(t, D_MODEL)
    qa, ka, va, hg, gt = _in_proj(xp, n1w, win, _tile(t, 256))
    oa = _attn_prompt(lam, qa, ka, va, sub_w, batch, seq, _tile(seq, 512), post_scale)
    ob, s_p = _hgrn_prompt(hg, lb, gn_w, batch, seq, _tile(seq, 128))
    y_p = tail(xp, oa, ob, gt, _tile(t, 256), _tile(t, 256), _tile(t, 512))

    xs = x_sample.reshape(n_dec, D_MODEL)
    qa_s, ka_s, va_s, hg_s, gt_s = _in_proj(xs, n1w, win, _tile(n_dec, 128))
    n_phys, page = cache_k.shape[1], cache_k.shape[2]
    oa_s = _attn_decode(page_table, lam, qa_s, ka_s, va_s, sub_w,
                        cache_k.reshape(n_phys, page, N_QK),
                        cache_v.reshape(n_phys, page, N_QK), post_scale)
    ob_s, s_s = _hgrn_sample(hg_s, lb, gn_w, state_hgrn.reshape(state_hgrn.shape[1:]),
                             _tile(n_dec, 8))
    y_s = tail(xs, oa_s, ob_s.reshape(n_dec, H_B * DV_B), gt_s,
               _tile(n_dec, 128), _tile(n_dec, 128), _tile(n_dec, 128))

    return (y_p.reshape(batch, seq, D_MODEL),
            y_s.reshape(n_dec, 1, D_MODEL),
            ka.reshape(1, batch, seq, H_A, 2, DH_QK),
            va.reshape(1, batch, seq, H_A, DV_A),
            s_p.reshape(1, batch, H_B, DK_B, DV_B),
            ka_s.reshape(1, n_dec, 1, H_A, 2, DH_QK),
            va_s.reshape(1, n_dec, 1, H_A, DV_A),
            s_s.reshape(1, n_dec, H_B, DK_B, DV_B))
```

```python
import functools
import math

import jax
import jax.numpy as jnp
from jax import lax
from jax.experimental import pallas as pl
from jax.experimental.pallas import tpu as pltpu

F32 = jnp.float32
BF16 = jnp.bfloat16

EPS = 1e-6
NEG = -1e30
LOG2E = math.log2(math.e)
LANES = 128
SUBLANES = 8
VMEM_LIMIT = 56 * 1024 * 1024

H_A = 4
DH_QK = 64
DV_A = 128
H_B = 4
DK_B = 128
DV_B = 128
N_KEYS = 128
H_P = 8
DK_P_HALF = 64
TOPK_P = 16
D_MODEL = 1024
N_QK = H_A * 2 * DH_QK
N_HG = 4 * H_B * DK_B
N_GT = 2 * D_MODEL
N_IN = 3 * N_QK + N_HG + N_GT


def _cparams(sem):
    return pltpu.CompilerParams(dimension_semantics=sem, vmem_limit_bytes=VMEM_LIMIT)


def _rms(x, w):
    return x * lax.rsqrt(jnp.mean(x * x, axis=-1, keepdims=True) + EPS) * w


def _sigmoid(x):
    return 1.0 / (1.0 + jnp.exp(-x))


def _silu(x):
    return x * _sigmoid(x)


def _dot(a, b):
    return jnp.dot(a, b, preferred_element_type=F32)


def _dot_nt(a, b):
    return lax.dot_general(a, b, (((1,), (1,)), ((), ())), preferred_element_type=F32)


def _inproj_kernel(x_ref, g_ref, w_ref, wkt_ref, qa_ref, ka_ref, kt_ref, va_ref, hg_ref, gt_ref):
    nb = _rms(x_ref[...], g_ref[...]).astype(BF16)
    cw = N_QK

    def mm(c):
        return _dot(nb, w_ref[:, c * cw:(c + 1) * cw])

    qa_ref[...] = (mm(0) * (DH_QK ** -0.5 * LOG2E)).astype(BF16)
    ka_ref[...] = mm(1).astype(BF16)
    kt_ref[0] = _dot_nt(wkt_ref[...], nb)
    va_ref[...] = mm(2)
    for c in range(N_HG // cw):
        hg_ref[:, c * cw:(c + 1) * cw] = mm(3 + c)
    for c in range(N_GT // cw):
        gt_ref[:, c * cw:(c + 1) * cw] = mm(3 + N_HG // cw + c)


def _in_proj(x, g, w, wkt, batch, tm):
    m = x.shape[0]
    seq = m // batch
    nt = seq // tm
    row = lambda n: pl.BlockSpec((tm, n), lambda b, i: (b * nt + i, 0))
    full = lambda a: pl.BlockSpec(a.shape, lambda b, i: (0, 0))
    return pl.pallas_call(
        _inproj_kernel,
        grid=(batch, nt),
        in_specs=[row(D_MODEL), full(g), full(w), full(wkt)],
        out_specs=[row(N_QK), row(N_QK), pl.BlockSpec((1, N_QK, tm), lambda b, i: (b, 0, i)),
                   row(N_QK), row(N_HG), row(N_GT)],
        out_shape=[jax.ShapeDtypeStruct((m, N_QK), BF16),
                   jax.ShapeDtypeStruct((m, N_QK), BF16),
                   jax.ShapeDtypeStruct((batch, N_QK, seq), F32),
                   jax.ShapeDtypeStruct((m, N_QK), F32),
                   jax.ShapeDtypeStruct((m, N_HG), F32),
                   jax.ShapeDtypeStruct((m, N_GT), F32)],
        compiler_params=_cparams(("parallel", "parallel")),
        name="in_proj",
    )(x, g, w, wkt)


def _attn_kernel(lam_ref, q_ref, k_ref, v_ref, w_ref, o_ref,
                 m1, l1, a1, m2, l2, a2, *, post_scale):
    qi = pl.program_id(2)
    ki = pl.program_id(3)
    tq = q_ref.shape[0]
    tk = k_ref.shape[0]

    @pl.when(ki == 0)
    def _():
        for m, l, a in ((m1, l1, a1), (m2, l2, a2)):
            m[...] = jnp.full(m.shape, NEG, F32)
            l[...] = jnp.zeros(l.shape, F32)
            a[...] = jnp.zeros(a.shape, F32)

    def step(masked):
        q = q_ref[...]
        lane = lax.broadcasted_iota(jnp.int32, q.shape, 1)
        k = k_ref[...]
        vt = jnp.transpose(v_ref[...]).astype(BF16)
        if masked:
            krow = lax.broadcasted_iota(jnp.int32, (tk, tq), 0)
            qcol = lax.broadcasted_iota(jnp.int32, (tk, tq), 1)
            keep = krow <= qcol
        for c, (m, l, a) in enumerate(((m1, l1, a1), (m2, l2, a2))):
            qc = jnp.where((lane >= DH_QK) == bool(c), q, jnp.zeros_like(q))
            s = _dot_nt(k, qc)
            if masked:
                s = jnp.where(keep, s, NEG)
            m_prev = m[...]
            m_new = jnp.maximum(m_prev, jnp.max(s, axis=0, keepdims=True))
            alpha = jnp.exp2(m_prev - m_new)
            p = jnp.exp2(s - m_new)
            l[...] = alpha * l[...] + jnp.sum(p, axis=0, keepdims=True)
            a[...] = alpha * a[...] + _dot(vt, p.astype(BF16))
            m[...] = m_new

    @pl.when(ki < qi)
    def _():
        step(False)

    @pl.when(ki == qi)
    def _():
        step(True)
        lam = lam_ref[0]
        ot = a1[...] / l1[...] - lam * (a2[...] / l2[...])
        o_ref[...] = _rms(jnp.transpose(ot), w_ref[...]) * post_scale


def _attn_prompt(lam, qa, ka, va, subln_w, batch, seq, tq, post_scale):
    nq = seq // tq
    qmap = lambda b, h, qi, ki: (b * nq + qi, h)
    kmap = lambda b, h, qi, ki: (b * nq + jnp.minimum(ki, qi), h)
    blk = lambda f: pl.BlockSpec((tq, DV_A), f)
    return pl.pallas_call(
        functools.partial(_attn_kernel, post_scale=post_scale),
        grid=(batch, H_A, nq, nq),
        in_specs=[pl.BlockSpec(memory_space=pltpu.SMEM),
                  blk(qmap), blk(kmap), blk(kmap),
                  pl.BlockSpec((1, DV_A), lambda b, h, qi, ki: (0, 0))],
        out_specs=blk(qmap),
        out_shape=jax.ShapeDtypeStruct((batch * seq, H_A * DV_A), F32),
        scratch_shapes=[pltpu.VMEM((1, tq), F32), pltpu.VMEM((1, tq), F32),
                        pltpu.VMEM((DV_A, tq), F32)] * 2,
        compiler_params=_cparams(("parallel", "parallel", "parallel", "arbitrary")),
        name="attn_prompt",
    )(lam, qa, ka, va, subln_w)


def _decode_kernel(pt_ref, lam_ref, q_ref, kn_ref, vn_ref, w_ref, *refs, n_pages, post_scale):
    k_refs = refs[:n_pages]
    v_refs = refs[n_pages:2 * n_pages]
    o_ref = refs[2 * n_pages]
    del pt_ref
    nmap = 2 * H_A
    width = N_QK
    page = k_refs[0].shape[2]

    q = q_ref[0].astype(F32)
    rows = lax.broadcasted_iota(jnp.int32, (nmap, width), 0)
    lane = lax.broadcasted_iota(jnp.int32, (nmap, width), 1)
    qrows = jnp.where(lane // DH_QK == rows, jnp.broadcast_to(q, (nmap, width)), 0.0)
    qb = qrows.astype(BF16)

    s_new = jnp.sum(qrows * kn_ref[0].astype(F32), axis=-1, keepdims=True)
    s = [_dot(qb, k_refs[i][0].astype(BF16)) for i in range(n_pages)]
    m = functools.reduce(jnp.maximum, s)
    m = jnp.maximum(jnp.max(m, axis=-1, keepdims=True), s_new)
    p = [jnp.exp2(si - m) for si in s]
    p_new = jnp.exp2(s_new - m)
    l = jnp.sum(functools.reduce(jnp.add, p), axis=-1, keepdims=True) + p_new

    r8 = lax.broadcasted_iota(jnp.int32, (nmap, page), 0)
    o8 = jnp.zeros((nmap, DV_A), F32)
    for i in range(n_pages):
        pb = p[i].astype(BF16)
        for h in range(H_A):
            vh = v_refs[i][0, pl.ds(h, page, stride=H_A), :].astype(BF16)
            o8 = o8 + _dot(jnp.where(r8 // 2 == h, pb, jnp.zeros_like(pb)), vh)

    vn = vn_ref[0].astype(BF16).astype(F32)
    r8v = lax.broadcasted_iota(jnp.int32, (nmap, DV_A), 0)
    vn8 = jnp.zeros((nmap, DV_A), F32)
    for h in range(H_A):
        vn8 = jnp.where(r8v // 2 == h, jnp.broadcast_to(vn[:, h * DV_A:(h + 1) * DV_A],
                                                        (nmap, DV_A)), vn8)
    o8 = (o8 + p_new.astype(BF16).astype(F32) * vn8) / l
    lam = lam_ref[0]
    w = w_ref[...]
    for h in range(H_A):
        d = o8[2 * h:2 * h + 1] - lam * o8[2 * h + 1:2 * h + 2]
        o_ref[0, :, h * DV_A:(h + 1) * DV_A] = _rms(d, w) * post_scale


def _attn_decode(page_table, lam, qa, ka, va, subln_w, cache_kt, cache_v, post_scale):
    nb, n_pages = page_table.shape
    width, page = cache_kt.shape[1], cache_kt.shape[2]
    one = lambda: pl.BlockSpec((1, 1, width), lambda b, pt: (b, 0, 0))
    kspec = lambda i: pl.BlockSpec((1, width, page), lambda b, pt: (pt[b, i], 0, 0))
    vspec = lambda i: pl.BlockSpec((1, page * H_A, DV_A), lambda b, pt: (pt[b, i], 0, 0))
    grid_spec = pltpu.PrefetchScalarGridSpec(
        num_scalar_prefetch=1,
        grid=(nb,),
        in_specs=[pl.BlockSpec(memory_space=pltpu.SMEM), one(), one(), one(),
                  pl.BlockSpec((1, DV_A), lambda b, pt: (0, 0))]
                 + [kspec(i) for i in range(n_pages)] + [vspec(i) for i in range(n_pages)],
        out_specs=one(),
    )
    r3 = lambda a: a.reshape(nb, 1, width)
    out = pl.pallas_call(
        functools.partial(_decode_kernel, n_pages=n_pages, post_scale=post_scale),
        grid_spec=grid_spec,
        out_shape=jax.ShapeDtypeStruct((nb, 1, width), F32),
        compiler_params=_cparams(("arbitrary",)),
        name="attn_decode",
    )(page_table, lam, r3(qa), r3(ka), r3(va), subln_w,
      *([cache_kt] * n_pages), *([cache_v] * n_pages))
    return out.reshape(nb, width)


def _hgrn_gates(qb, fb, lb):
    q = _silu(qb) * (DK_B ** -0.5)
    f = lb + (1.0 - lb) * _sigmoid(fb)
    return q, 1.0 - f, jnp.log(f)


def _hgrn_post(o, gb, w):
    return _rms(o, w) * _silu(gb)


def _hgrn_prompt_kernel(qb_ref, fb_ref, ib_ref, gb_ref, lb_ref, w_ref, o_ref, s_ref, st_ref):
    ci = pl.program_id(2)
    c = qb_ref.shape[0]
    dk = DK_B

    @pl.when(ci == 0)
    def _():
        st_ref[...] = jnp.zeros(st_ref.shape, F32)

    q, k, lf = _hgrn_gates(qb_ref[...], fb_ref[...], lb_ref[0])
    v = ib_ref[...]
    vb = v.astype(BF16)

    row = lax.broadcasted_iota(jnp.int32, (c, c), 0)
    col = lax.broadcasted_iota(jnp.int32, (c, c), 1)
    tril = jnp.where(col <= row, 1.0, 0.0).astype(BF16)
    hi = lf.astype(BF16)
    r1 = lf - hi.astype(F32)
    mid = r1.astype(BF16)
    lo = (r1 - mid.astype(F32)).astype(BF16)
    g = _dot(tril, hi) + _dot(tril, mid) + _dot(tril, lo)

    st = st_ref[...]
    o = _dot_nt((q * jnp.exp(g)).astype(BF16), st.astype(BF16))

    a = jnp.zeros((c, c), F32)
    m = SUBLANES
    while 2 * m <= c:
        nb = c // (2 * m)
        g3 = g.reshape(nb, 2 * m, dk)
        d = g3 - g3[:, m - 1:m, :]
        rin = lax.broadcasted_iota(jnp.int32, (nb, 2 * m, dk), 1)
        qs = jnp.where(rin >= m, q.reshape(nb, 2 * m, dk) * jnp.exp(jnp.minimum(d, 0.0)), 0.0)
        ks = jnp.where(rin < m, k.reshape(nb, 2 * m, dk) * jnp.exp(jnp.minimum(-d, 0.0)), 0.0)
        al = _dot_nt(qs.reshape(c, dk).astype(BF16), ks.reshape(c, dk).astype(BF16))
        a = a + jnp.where(row // (2 * m) == col // (2 * m), al, 0.0)
        m *= 2

    nb = c // SUBLANES
    g8 = g.reshape(nb, SUBLANES, dk)
    q8 = q.reshape(nb, SUBLANES, dk)
    k8 = k.reshape(nb, SUBLANES, dk)
    rin = lax.broadcasted_iota(jnp.int32, (nb, SUBLANES, dk), 1)
    ones = jnp.ones((dk, c), BF16)
    for s in range(SUBLANES):
        d = g8 - g8[:, s:s + 1, :]
        p = jnp.where(rin >= s, q8 * k8[:, s:s + 1, :] * jnp.exp(jnp.minimum(d, 0.0)), 0.0)
        r = _dot(p.reshape(c, dk).astype(BF16), ones)
        a = a + jnp.where(col == (row // SUBLANES) * SUBLANES + s, r, 0.0)

    o = o + _dot(a.astype(BF16), vb)
    o_ref[...] = _hgrn_post(o, gb_ref[...], w_ref[...])

    g_end = g[c - 1:c, :]
    kd = (k * jnp.exp(g_end - g)).astype(BF16)
    st_new = st * jnp.exp(g_end) + _dot(jnp.transpose(v).astype(BF16), kd)
    st_ref[...] = st_new

    @pl.when(ci == pl.num_programs(2) - 1)
    def _():
        s_ref[0, 0] = jnp.transpose(st_new)


def _hgrn_prompt(hg, lb, gnorm_w, batch, seq, chunk):
    nc = seq // chunk
    cmap = lambda j: (lambda b, h, c: (b * nc + c, j * H_B + h))
    blk = lambda j: pl.BlockSpec((chunk, DK_B), cmap(j))
    return pl.pallas_call(
        _hgrn_prompt_kernel,
        grid=(batch, H_B, nc),
        in_specs=[blk(0), blk(1), blk(2), blk(3),
                  pl.BlockSpec((1, 1, DK_B), lambda b, h, c: (h, 0, 0)),
                  pl.BlockSpec((1, DV_B), lambda b, h, c: (0, 0))],
        out_specs=[pl.BlockSpec((chunk, DV_B), lambda b, h, c: (b * nc + c, h)),
                   pl.BlockSpec((1, 1, DK_B, DV_B), lambda b, h, c: (b, h, 0, 0))],
        out_shape=[jax.ShapeDtypeStruct((batch * seq, H_B * DV_B), F32),
                   jax.ShapeDtypeStruct((batch, H_B, DK_B, DV_B), F32)],
        scratch_shapes=[pltpu.VMEM((DV_B, DK_B), F32)],
        compiler_params=_cparams(("parallel", "parallel", "arbitrary")),
        name="hgrn_prompt",
    )(hg, hg, hg, hg, lb, gnorm_w)


def _hgrn_sample_kernel(hg_ref, lb_ref, w_ref, s0_ref, o_ref, s_ref):
    nb = s0_ref.shape[0]
    for b in range(nb):
        for h in range(H_B):
            sl = lambda j: slice((j * H_B + h) * DK_B, (j * H_B + h + 1) * DK_B)
            row = lambda j: hg_ref[b, :, sl(j)]
            q, k, lf = _hgrn_gates(row(0), row(1), lb_ref[h])
            v = row(2)
            stack = jnp.concatenate(
                [q, k, jnp.exp(lf), jnp.zeros((DK_B - 3, DK_B), F32)], axis=0)
            cols = jnp.transpose(stack)
            s_new = cols[:, 2:3] * s0_ref[b, h] + cols[:, 1:2] * v
            s_ref[b, h] = s_new
            o = jnp.sum(cols[:, 0:1] * s_new, axis=0, keepdims=True)
            o_ref[b, :, h * DV_B:(h + 1) * DV_B] = _hgrn_post(o, row(3), w_ref[...])


def _hgrn_sample(hg, lb, gnorm_w, state, tb):
    nb = state.shape[0]
    return pl.pallas_call(
        _hgrn_sample_kernel,
        grid=(nb // tb,),
        in_specs=[pl.BlockSpec((tb, 1, N_HG), lambda i: (i, 0, 0)),
                  pl.BlockSpec((H_B, 1, DK_B), lambda i: (0, 0, 0)),
                  pl.BlockSpec((1, DV_B), lambda i: (0, 0)),
                  pl.BlockSpec((tb, H_B, DK_B, DV_B), lambda i: (i, 0, 0, 0))],
        out_specs=[pl.BlockSpec((tb, 1, H_B * DV_B), lambda i: (i, 0, 0)),
                   pl.BlockSpec((tb, H_B, DK_B, DV_B), lambda i: (i, 0, 0, 0))],
        out_shape=[jax.ShapeDtypeStruct((nb, 1, H_B * DV_B), F32),
                   jax.ShapeDtypeStruct(state.shape, F32)],
        compiler_params=_cparams(("parallel",)),
        name="hgrn_sample",
    )(hg.reshape(nb, 1, N_HG), lb, gnorm_w, state)


def _merge_kernel(x_ref, oa_ref, ob_ref, gt_ref, wa_ref, wb_ref, wo_ref, n2w_ref, wq_ref,
                  keys_ref, h_ref, nt_ref, s1_ref, s2_ref):
    ga = gt_ref[:, :D_MODEL]
    gb = gt_ref[:, D_MODEL:]
    m = (_sigmoid(ga) * _dot(oa_ref[...].astype(BF16), wa_ref[...])
         + _sigmoid(gb) * _dot(ob_ref[...].astype(BF16), wb_ref[...]))
    h = x_ref[...] + _dot(m.astype(BF16), wo_ref[...])
    h_ref[...] = h
    n2 = _rms(h, n2w_ref[...])
    nt_ref[...] = jnp.transpose(n2).astype(BF16)
    qp = _dot(n2.astype(BF16), wq_ref[...]).astype(BF16)
    s1_ref[...] = _dot_nt(keys_ref[0], qp)
    s2_ref[...] = _dot_nt(keys_ref[1], qp)


def _merge(x, oa, ob, gt, wa, wb, wo, n2w, wq, keys, tm):
    t = x.shape[0]
    row = lambda n: pl.BlockSpec((tm, n), lambda i: (i, 0))
    full = lambda a: pl.BlockSpec(a.shape, lambda i: (0,) * a.ndim)
    sspec = pl.BlockSpec((N_KEYS * H_P, tm), lambda i: (0, i))
    return pl.pallas_call(
        _merge_kernel,
        grid=(t // tm,),
        in_specs=[row(D_MODEL), row(N_QK), row(H_B * DV_B), row(N_GT),
                  full(wa), full(wb), full(wo), full(n2w), full(wq), full(keys)],
        out_specs=[row(D_MODEL), pl.BlockSpec((D_MODEL, tm), lambda i: (0, i)), sspec, sspec],
        out_shape=[jax.ShapeDtypeStruct((t, D_MODEL), F32),
                   jax.ShapeDtypeStruct((D_MODEL, t), BF16),
                   jax.ShapeDtypeStruct((N_KEYS * H_P, t), F32),
                   jax.ShapeDtypeStruct((N_KEYS * H_P, t), F32)],
        compiler_params=_cparams(("parallel",)),
        name="merge",
    )(x, oa, ob, gt, wa, wb, wo, n2w, wq, keys)


def _bitonic_merge_desc(a):
    a = list(a)
    d = len(a) // 2
    while d >= 1:
        for i in range(len(a)):
            if i & d == 0:
                a[i], a[i + d] = jnp.maximum(a[i], a[i + d]), jnp.minimum(a[i], a[i + d])
        d //= 2
    return a


def _sort_desc(a):
    if len(a) == 1:
        return list(a)
    half = len(a) // 2
    return _bitonic_merge_desc(_sort_desc(a[:half]) + _sort_desc(a[half:])[::-1])


def _top16(vals):
    groups = [_sort_desc(vals[i:i + TOPK_P]) for i in range(0, len(vals), TOPK_P)]
    while len(groups) > 1:
        groups = [_bitonic_merge_desc([jnp.maximum(a[i], b[TOPK_P - 1 - i])
                                       for i in range(TOPK_P)])
                  for a, b in zip(groups[0::2], groups[1::2])]
    return groups[0]


def _topk_kernel(s1_ref, s2_ref, cnt_ref, e1_ref, r2_ref, e2_ref):
    tl = s1_ref.shape[1]
    key_rows = lambda ref, e: ref[pl.ds(e * H_P, H_P), :]
    t1 = _top16([key_rows(s1_ref, e) for e in range(N_KEYS)])
    t2 = _top16([key_rows(s2_ref, e) for e in range(N_KEYS)])
    pairs = [(p, r) for p in range(TOPK_P) for r in range(TOPK_P) if (p + 1) * (r + 1) <= TOPK_P]
    sums = {pr: t1[pr[0]] + t2[pr[1]] for pr in pairs}
    pad = [jnp.full((H_P, tl), NEG, F32)] * (-len(pairs) % (2 * TOPK_P))
    top = _top16([sums[pr] for pr in pairs] + pad)
    tau = top[TOPK_P - 1]
    z = functools.reduce(jnp.add, [jnp.exp(c - top[0]) for c in top])
    theta = []
    for r in range(TOPK_P):
        th = jnp.full((H_P, tl), -NEG, F32)
        for p in range(TOPK_P // (r + 1)):
            th = jnp.where(sums[(p, r)] >= tau, t1[p], th)
        theta.append(th)
    shift1 = t1[0] + jnp.log(z)
    for e in range(N_KEYS):
        x = key_rows(s1_ref, e)
        cnt = jnp.zeros((H_P, tl), F32)
        for r in range(TOPK_P):
            cnt = cnt + jnp.where(x >= theta[r], 1.0, 0.0)
        cnt_ref[pl.ds(e * H_P, H_P), :] = cnt
        e1_ref[pl.ds(e * H_P, H_P), :] = jnp.exp(x - shift1)
    for h in range(H_P):
        xh = s2_ref[pl.ds(h, N_KEYS, stride=H_P), :]
        rank = jnp.zeros((N_KEYS, tl), F32)
        for r in range(TOPK_P):
            rank = rank + jnp.where(xh < t2[r][h:h + 1, :], 1.0, 0.0)
        r2_ref[h] = rank.astype(BF16)
        e2_ref[h] = jnp.exp(xh - t2[0][h:h + 1, :]).astype(BF16)


def _topk(s1, s2, tl):
    t = s1.shape[1]
    sspec = pl.BlockSpec((N_KEYS * H_P, tl), lambda i: (0, i))
    hspec = pl.BlockSpec((H_P, N_KEYS, tl), lambda i: (0, 0, i))
    return pl.pallas_call(
        _topk_kernel,
        grid=(t // tl,),
        in_specs=[sspec, sspec],
        out_specs=[sspec, sspec, hspec, hspec],
        out_shape=[jax.ShapeDtypeStruct((N_KEYS * H_P, t), F32),
                   jax.ShapeDtypeStruct((N_KEYS * H_P, t), F32),
                   jax.ShapeDtypeStruct((H_P, N_KEYS, t), BF16),
                   jax.ShapeDtypeStruct((H_P, N_KEYS, t), BF16)],
        compiler_params=_cparams(("parallel",)),
        name="topk",
    )(s1, s2)


def _gelu(x):
    return 0.5 * x * (1.0 + lax.erf(x * (2.0 ** -0.5)))


def _peer_kernel(xt_ref, u_ref, v_ref, cnt_ref, e1_ref, r2_ref, e2_ref, h_ref, nfw_ref,
                 y_ref, acc_ref, *, e1_per_step):
    j = pl.program_id(1)

    @pl.when(j == 0)
    def _():
        acc_ref[...] = jnp.zeros(acc_ref.shape, F32)

    ws = []
    for a in range(e1_per_step):
        e1 = j * e1_per_step + a
        w = None
        for hh in range(H_P):
            row = pl.ds(e1 * H_P + hh, 1)
            cnt = cnt_ref[row, :].astype(BF16)
            e2 = e2_ref[hh]
            term = jnp.where(r2_ref[hh] < cnt, e2, jnp.zeros_like(e2)) * e1_ref[row, :].astype(BF16)
            w = term if w is None else w + term
        ws.append(w.astype(F32))
    w = jnp.concatenate(ws, axis=0)
    act = _gelu(_dot(u_ref[...], xt_ref[...]))
    acc_ref[...] += _dot(jnp.transpose(w * act).astype(BF16), v_ref[...])

    @pl.when(j == pl.num_programs(1) - 1)
    def _():
        y_ref[...] = _rms(h_ref[...] + acc_ref[...], nfw_ref[...])


def _peer(xt, u, v, cnt, e1w, r2, e2w, h, nfw, tm, e1_per_step):
    t = h.shape[0]
    te = e1_per_step * N_KEYS
    kspec = pl.BlockSpec((N_KEYS * H_P, tm), lambda i, j: (0, i))
    hspec = pl.BlockSpec((H_P, N_KEYS, tm), lambda i, j: (0, 0, i))
    row = pl.BlockSpec((tm, D_MODEL), lambda i, j: (i, 0))
    return pl.pallas_call(
        functools.partial(_peer_kernel, e1_per_step=e1_per_step),
        grid=(t // tm, N_KEYS // e1_per_step),
        in_specs=[pl.BlockSpec((D_MODEL, tm), lambda i, j: (0, i)),
                  pl.BlockSpec((te, D_MODEL), lambda i, j: (j, 0)),
                  pl.BlockSpec((te, D_MODEL), lambda i, j: (j, 0)),
                  kspec, kspec, hspec, hspec, row,
                  pl.BlockSpec((1, D_MODEL), lambda i, j: (0, 0))],
        out_specs=row,
        out_shape=jax.ShapeDtypeStruct((t, D_MODEL), F32),
        scratch_shapes=[pltpu.VMEM((tm, D_MODEL), F32)],
        compiler_params=_cparams(("parallel", "arbitrary")),
        name="peer",
    )(xt, u, v, cnt, e1w, r2, e2w, h, nfw)


def _tile(n, pref):
    t = min(n, pref)
    assert n % t == 0, (n, t)
    return t


def kernel(x_prompt, x_sample, cache_k, cache_v, state_hgrn, page_table, norm1_w, w_in,
           lambda_q1, lambda_k1, lambda_q2, lambda_k2, subln_w, lb_param, gnorm_w,
           w_branch_a, w_branch_b, w_out, norm2_w, w_query, sub_keys, expert_u, expert_v,
           norm_f_w):
    depth = w_in.shape[0]
    assert depth == 1 and w_in.shape[2] == N_IN
    batch, seq, _ = x_prompt.shape
    n_dec, dec_len, _ = x_sample.shape
    assert dec_len == 1
    l = 0

    lam_init = 0.8 - 0.6 * math.exp(-0.3 * l)
    post_scale = 1.0 - lam_init
    f = F32
    lam = (jnp.exp(jnp.sum(lambda_q1[l].astype(f) * lambda_k1[l].astype(f)))
           - jnp.exp(jnp.sum(lambda_q2[l].astype(f) * lambda_k2[l].astype(f)))
           + lam_init).reshape(1)
    lb = jnp.cumsum(jax.nn.softmax(lb_param.astype(f), axis=0), axis=0)[l].reshape(H_B, 1, DK_B)

    n1w = norm1_w[l].reshape(1, D_MODEL)
    n2w = norm2_w[l].reshape(1, D_MODEL)
    nfw = norm_f_w.reshape(1, D_MODEL)
    sub_w = subln_w[l].reshape(1, DV_A)
    gn_w = gnorm_w[l].reshape(1, DV_B)
    win = w_in[l].astype(BF16)
    wa = w_branch_a[l].astype(BF16)
    wb = w_branch_b[l].astype(BF16)
    wo = w_out[l].astype(BF16)
    wq = w_query[l].astype(BF16)
    sk = jnp.transpose(sub_keys[l].astype(BF16), (1, 2, 0, 3))
    eye_h = jnp.eye(H_P, dtype=BF16)
    eye_c = jnp.eye(2, dtype=BF16)
    keys = jnp.einsum('cnhd,hg,cb->cnhgbd', sk, eye_h, eye_c).reshape(
        2, N_KEYS * H_P, H_P * 2 * DK_P_HALF)
    u = expert_u.reshape(expert_u.shape[1:]).astype(BF16)
    v = expert_v.reshape(expert_v.shape[1:]).astype(BF16)

    def tail(x2, oa, ob, gt, tm_merge, tl, tm_peer, e1_per_step=4):
        h, nt, s1, s2 = _merge(x2, oa, ob, gt, wa, wb, wo, n2w, wq, keys, tm_merge)
        cnt, e1w, r2, e2w = _topk(s1, s2, tl)
        return _peer(nt, u, v, cnt, e1w, r2, e2w, h, nfw, tm_peer, e1_per_step)

    wkt = jnp.transpose(w_in[l, :, N_QK:2 * N_QK]).astype(BF16)

    def keys_out(kt, b, s):
        return jnp.transpose(kt.reshape(1, b, H_A, 2, DH_QK, s), (0, 1, 5, 2, 3, 4))

    t = batch * seq
    xp = x_prompt.reshape(t, D_MODEL)
    qa, ka, kt, va, hg, gt = _in_proj(xp, n1w, win, wkt, batch, _tile(seq, 256))
    oa = _attn_prompt(lam, qa, ka, va, sub_w, batch, seq, _tile(seq, 512), post_scale)
    ob, s_p = _hgrn_prompt(hg, lb, gn_w, batch, seq, _tile(seq, 128))
    y_p = tail(xp, oa, ob, gt, _tile(t, 256), _tile(t, 128), _tile(t, 512))

    xs = x_sample.reshape(n_dec, D_MODEL)
    qa_s, ka_s, kt_s, va_s, hg_s, gt_s = _in_proj(xs, n1w, win, wkt, 1, _tile(n_dec, 128))
    n_phys, page = cache_k.shape[1], cache_k.shape[2]
    cache_kt = jnp.transpose(cache_k, (0, 1, 3, 4, 5, 2)).reshape(n_phys, N_QK, page)
    cache_v2 = cache_v.reshape(n_phys, page * H_A, DV_A)
    oa_s = _attn_decode(page_table, lam, qa_s, ka_s, va_s, sub_w, cache_kt, cache_v2, post_scale)
    ob_s, s_s = _hgrn_sample(hg_s, lb, gn_w, state_hgrn.reshape(state_hgrn.shape[1:]),
                             _tile(n_dec, 8))
    y_s = tail(xs, oa_s, ob_s.reshape(n_dec, H_B * DV_B), gt_s,
               _tile(n_dec, 128), _tile(n_dec, 128), _tile(n_dec, 128))

    return (y_p.reshape(batch, seq, D_MODEL),
            y_s.reshape(n_dec, 1, D_MODEL),
            keys_out(kt, batch, seq),
            va.reshape(1, batch, seq, H_A, DV_A),
            s_p.reshape(1, batch, H_B, DK_B, DV_B),
            jnp.transpose(keys_out(kt_s, 1, n_dec), (0, 2, 1, 3, 4, 5)),
            va_s.reshape(1, n_dec, 1, H_A, DV_A),
            s_s.reshape(1, n_dec, H_B, DK_B, DV_B))
```

```python
import functools
import math

import jax
import jax.numpy as jnp
from jax import lax
from jax.experimental import pallas as pl
from jax.experimental.pallas import tpu as pltpu

F32 = jnp.float32
BF16 = jnp.bfloat16

EPS = 1e-6
NEG = -1e30
LOG2E = math.log2(math.e)
LANES = 128
SUBLANES = 8
VMEM_LIMIT = 56 * 1024 * 1024

H_A = 4
DH_QK = 64
DV_A = 128
H_B = 4
DK_B = 128
DV_B = 128
N_KEYS = 128
H_P = 8
DK_P_HALF = 64
TOPK_P = 16
D_MODEL = 1024
N_QK = H_A * 2 * DH_QK
N_HG = 4 * H_B * DK_B
N_GT = 2 * D_MODEL
N_IN = 3 * N_QK + N_HG + N_GT


def _cparams(sem):
    return pltpu.CompilerParams(dimension_semantics=sem, vmem_limit_bytes=VMEM_LIMIT)


def _rms(x, w):
    return x * lax.rsqrt(jnp.mean(x * x, axis=-1, keepdims=True) + EPS) * w


def _sigmoid(x):
    return 1.0 / (1.0 + jnp.exp(-x))


def _silu(x):
    return x * _sigmoid(x)


def _dot(a, b):
    return jnp.dot(a, b, preferred_element_type=F32)


def _dot_nt(a, b):
    return lax.dot_general(a, b, (((1,), (1,)), ((), ())), preferred_element_type=F32)


def _inproj_kernel(x_ref, g_ref, w_ref, wkvt_ref, qa_ref, ka_ref, kt_ref, va_ref, vt_ref,
                   hg_ref, gt_ref):
    nb = _rms(x_ref[...], g_ref[...]).astype(BF16)
    cw = N_QK

    def mm(c):
        return _dot(nb, w_ref[:, c * cw:(c + 1) * cw])

    qa_ref[...] = (mm(0) * (DH_QK ** -0.5 * LOG2E)).astype(BF16)
    ka_ref[...] = mm(1).astype(BF16)
    kvt = _dot_nt(wkvt_ref[...], nb)
    kt_ref[0] = kvt[:N_QK]
    vt_ref[0] = kvt[N_QK:].astype(BF16)
    va_ref[...] = mm(2)
    for c in range(N_HG // cw):
        hg_ref[:, c * cw:(c + 1) * cw] = mm(3 + c)
    for c in range(N_GT // cw):
        gt_ref[:, c * cw:(c + 1) * cw] = mm(3 + N_HG // cw + c)


def _in_proj(x, g, w, wkvt, batch, tm):
    m = x.shape[0]
    seq = m // batch
    nt = seq // tm
    row = lambda n: pl.BlockSpec((tm, n), lambda b, i: (b * nt + i, 0))
    full = lambda a: pl.BlockSpec(a.shape, lambda b, i: (0, 0))
    tspec = pl.BlockSpec((1, N_QK, tm), lambda b, i: (b, 0, i))
    return pl.pallas_call(
        _inproj_kernel,
        grid=(batch, nt),
        in_specs=[row(D_MODEL), full(g), full(w), full(wkvt)],
        out_specs=[row(N_QK), row(N_QK), tspec, row(N_QK), tspec, row(N_HG), row(N_GT)],
        out_shape=[jax.ShapeDtypeStruct((m, N_QK), BF16),
                   jax.ShapeDtypeStruct((m, N_QK), BF16),
                   jax.ShapeDtypeStruct((batch, N_QK, seq), F32),
                   jax.ShapeDtypeStruct((m, N_QK), F32),
                   jax.ShapeDtypeStruct((batch, N_QK, seq), BF16),
                   jax.ShapeDtypeStruct((m, N_HG), F32),
                   jax.ShapeDtypeStruct((m, N_GT), F32)],
        compiler_params=_cparams(("parallel", "parallel")),
        name="in_proj",
    )(x, g, w, wkvt)


def _attn_kernel(lam_ref, q_ref, k_ref, vt_ref, w_ref, o_ref,
                 m1, l1, a1, m2, l2, a2, *, tk, post_scale):
    qi = pl.program_id(2)
    tq = q_ref.shape[0]

    for m, l, a in ((m1, l1, a1), (m2, l2, a2)):
        m[...] = jnp.full(m.shape, NEG, F32)
        l[...] = jnp.zeros(l.shape, F32)
        a[...] = jnp.zeros(a.shape, F32)

    q = q_ref[...]
    lane = lax.broadcasted_iota(jnp.int32, q.shape, 1)
    qcs = [jnp.where((lane >= DH_QK) == bool(c), q, jnp.zeros_like(q)) for c in range(2)]

    def step(start, diag_offset):
        start = pl.multiple_of(start, tk)
        k = k_ref[pl.ds(start, tk), :]
        vt = vt_ref[0, :, pl.ds(start, tk)]
        if diag_offset is not None:
            krow = lax.broadcasted_iota(jnp.int32, (tk, tq), 0) + diag_offset
            qcol = lax.broadcasted_iota(jnp.int32, (tk, tq), 1)
            keep = krow <= qcol
        for qc, (m, l, a) in zip(qcs, ((m1, l1, a1), (m2, l2, a2))):
            s = _dot_nt(k, qc)
            if diag_offset is not None:
                s = jnp.where(keep, s, NEG)
            m_prev = m[...]
            m_new = jnp.maximum(m_prev, jnp.max(s, axis=0, keepdims=True))
            alpha = jnp.exp2(m_prev - m_new)
            p = jnp.exp2(s - m_new)
            l[...] = alpha * l[...] + jnp.sum(p, axis=0, keepdims=True)
            a[...] = alpha * a[...] + _dot(vt, p.astype(BF16))
            m[...] = m_new

    def body(ki, carry):
        step(ki * tk, None)
        return carry

    lax.fori_loop(0, qi * (tq // tk), body, 0)
    for j in range(tq // tk):
        step(qi * tq + j * tk, j * tk)

    lam = lam_ref[0]
    ot = a1[...] / l1[...] - lam * (a2[...] / l2[...])
    o_ref[...] = _rms(jnp.transpose(ot), w_ref[...]) * post_scale


def _attn_prompt(lam, qa, ka, vat, subln_w, batch, seq, tq, tk, post_scale):
    nq = seq // tq
    return pl.pallas_call(
        functools.partial(_attn_kernel, tk=tk, post_scale=post_scale),
        grid=(batch, H_A, nq),
        in_specs=[pl.BlockSpec(memory_space=pltpu.SMEM),
                  pl.BlockSpec((tq, DV_A), lambda b, h, qi: (b * nq + qi, h)),
                  pl.BlockSpec((seq, DV_A), lambda b, h, qi: (b, h)),
                  pl.BlockSpec((1, DV_A, seq), lambda b, h, qi: (b, h, 0)),
                  pl.BlockSpec((1, DV_A), lambda b, h, qi: (0, 0))],
        out_specs=pl.BlockSpec((tq, DV_A), lambda b, h, qi: (b * nq + qi, h)),
        out_shape=jax.ShapeDtypeStruct((batch * seq, H_A * DV_A), F32),
        scratch_shapes=[pltpu.VMEM((1, tq), F32), pltpu.VMEM((1, tq), F32),
                        pltpu.VMEM((DV_A, tq), F32)] * 2,
        compiler_params=_cparams(("parallel", "parallel", "arbitrary")),
        name="attn_prompt",
    )(lam, qa, ka, vat, subln_w)


def _decode_kernel(pt_ref, lam_ref, q_ref, kn_ref, vn_ref, w_ref, *refs, n_pages, post_scale):
    k_refs = refs[:n_pages]
    v_refs = refs[n_pages:2 * n_pages]
    o_ref = refs[2 * n_pages]
    del pt_ref
    nmap = 2 * H_A
    width = N_QK
    page = k_refs[0].shape[2]

    q = q_ref[0].astype(F32)
    rows = lax.broadcasted_iota(jnp.int32, (nmap, width), 0)
    lane = lax.broadcasted_iota(jnp.int32, (nmap, width), 1)
    qrows = jnp.where(lane // DH_QK == rows, jnp.broadcast_to(q, (nmap, width)), 0.0)
    qb = qrows.astype(BF16)

    s_new = jnp.sum(qrows * kn_ref[0].astype(F32), axis=-1, keepdims=True)
    s = [_dot(qb, k_refs[i][0].astype(BF16)) for i in range(n_pages)]
    m = functools.reduce(jnp.maximum, s)
    m = jnp.maximum(jnp.max(m, axis=-1, keepdims=True), s_new)
    p = [jnp.exp2(si - m) for si in s]
    p_new = jnp.exp2(s_new - m)
    l = jnp.sum(functools.reduce(jnp.add, p), axis=-1, keepdims=True) + p_new

    r8 = lax.broadcasted_iota(jnp.int32, (nmap, page), 0)
    o8 = jnp.zeros((nmap, DV_A), F32)
    for i in range(n_pages):
        pb = p[i].astype(BF16)
        for h in range(H_A):
            vh = v_refs[i][0, pl.ds(h, page, stride=H_A), :].astype(BF16)
            o8 = o8 + _dot(jnp.where(r8 // 2 == h, pb, jnp.zeros_like(pb)), vh)

    vn = vn_ref[0].astype(BF16).astype(F32)
    r8v = lax.broadcasted_iota(jnp.int32, (nmap, DV_A), 0)
    vn8 = jnp.zeros((nmap, DV_A), F32)
    for h in range(H_A):
        vn8 = jnp.where(r8v // 2 == h, jnp.broadcast_to(vn[:, h * DV_A:(h + 1) * DV_A],
                                                        (nmap, DV_A)), vn8)
    o8 = (o8 + p_new.astype(BF16).astype(F32) * vn8) / l
    lam = lam_ref[0]
    w = w_ref[...]
    for h in range(H_A):
        d = o8[2 * h:2 * h + 1] - lam * o8[2 * h + 1:2 * h + 2]
        o_ref[0, :, h * DV_A:(h + 1) * DV_A] = _rms(d, w) * post_scale


def _attn_decode(page_table, lam, qa, ka, va, subln_w, cache_kt, cache_v, post_scale):
    nb, n_pages = page_table.shape
    width, page = cache_kt.shape[1], cache_kt.shape[2]
    one = lambda: pl.BlockSpec((1, 1, width), lambda b, pt: (b, 0, 0))
    kspec = lambda i: pl.BlockSpec((1, width, page), lambda b, pt: (pt[b, i], 0, 0))
    vspec = lambda i: pl.BlockSpec((1, page * H_A, DV_A), lambda b, pt: (pt[b, i], 0, 0))
    grid_spec = pltpu.PrefetchScalarGridSpec(
        num_scalar_prefetch=1,
        grid=(nb,),
        in_specs=[pl.BlockSpec(memory_space=pltpu.SMEM), one(), one(), one(),
                  pl.BlockSpec((1, DV_A), lambda b, pt: (0, 0))]
                 + [kspec(i) for i in range(n_pages)] + [vspec(i) for i in range(n_pages)],
        out_specs=one(),
    )
    r3 = lambda a: a.reshape(nb, 1, width)
    out = pl.pallas_call(
        functools.partial(_decode_kernel, n_pages=n_pages, post_scale=post_scale),
        grid_spec=grid_spec,
        out_shape=jax.ShapeDtypeStruct((nb, 1, width), F32),
        compiler_params=_cparams(("arbitrary",)),
        name="attn_decode",
    )(page_table, lam, r3(qa), r3(ka), r3(va), subln_w,
      *([cache_kt] * n_pages), *([cache_v] * n_pages))
    return out.reshape(nb, width)


def _hgrn_gates(qb, fb, lb):
    q = _silu(qb) * (DK_B ** -0.5)
    f = lb + (1.0 - lb) * _sigmoid(fb)
    return q, 1.0 - f, jnp.log(f)


def _hgrn_post(o, gb, w):
    return _rms(o, w) * _silu(gb)


def _hgrn_prompt_kernel(qb_ref, fb_ref, ib_ref, gb_ref, lb_ref, w_ref, o_ref, s_ref, *, chunk):
    c = chunk
    dk = DK_B
    seq = qb_ref.shape[0]
    heads = lb_ref.shape[0]
    gw = w_ref[...]

    row = lax.broadcasted_iota(jnp.int32, (c, c), 0)
    col = lax.broadcasted_iota(jnp.int32, (c, c), 1)
    tril = jnp.where(col <= row, 1.0, 0.0).astype(BF16)
    ones = jnp.ones((dk, c), BF16)
    levels = []
    m = SUBLANES
    while 2 * m <= c:
        levels.append((m, jnp.where(row // (2 * m) == col // (2 * m), 1.0, 0.0)))
        m *= 2
    diag = [jnp.where(col == (row // SUBLANES) * SUBLANES + s, 1.0, 0.0) for s in range(SUBLANES)]

    def one_head(ci, st, hh):
        rows = pl.ds(pl.multiple_of(ci * c, c), c)
        lanes = slice(hh * dk, (hh + 1) * dk)
        q, k, lf = _hgrn_gates(qb_ref[rows, lanes], fb_ref[rows, lanes], lb_ref[hh])
        v = ib_ref[rows, lanes]

        hi = lf.astype(BF16)
        r1 = lf - hi.astype(F32)
        mid = r1.astype(BF16)
        lo = (r1 - mid.astype(F32)).astype(BF16)
        g = _dot(tril, hi) + _dot(tril, mid) + _dot(tril, lo)

        o = _dot_nt((q * jnp.exp(g)).astype(BF16), st.astype(BF16))

        a = jnp.zeros((c, c), F32)
        for m, same_block in levels:
            nb = c // (2 * m)
            g3 = g.reshape(nb, 2 * m, dk)
            d = g3 - g3[:, m - 1:m, :]
            rin = lax.broadcasted_iota(jnp.int32, (nb, 2 * m, dk), 1)
            qs = jnp.where(rin >= m, q.reshape(nb, 2 * m, dk) * jnp.exp(jnp.minimum(d, 0.0)), 0.0)
            ks = jnp.where(rin < m, k.reshape(nb, 2 * m, dk) * jnp.exp(jnp.minimum(-d, 0.0)), 0.0)
            al = _dot_nt(qs.reshape(c, dk).astype(BF16), ks.reshape(c, dk).astype(BF16))
            a = a + al * same_block

        nb = c // SUBLANES
        g8 = g.reshape(nb, SUBLANES, dk)
        q8 = q.reshape(nb, SUBLANES, dk)
        k8 = k.reshape(nb, SUBLANES, dk)
        rin = lax.broadcasted_iota(jnp.int32, (nb, SUBLANES, dk), 1)
        for s in range(SUBLANES):
            d = g8 - g8[:, s:s + 1, :]
            p = jnp.where(rin >= s, q8 * k8[:, s:s + 1, :] * jnp.exp(jnp.minimum(d, 0.0)), 0.0)
            r = _dot(p.reshape(c, dk).astype(BF16), ones)
            a = a + r * diag[s]

        o = o + _dot(a.astype(BF16), v.astype(BF16))
        o_ref[rows, lanes] = _hgrn_post(o, gb_ref[rows, lanes], gw)

        g_end = g[c - 1:c, :]
        kd = (k * jnp.exp(g_end - g)).astype(BF16)
        return st * jnp.exp(g_end) + _dot(jnp.transpose(v).astype(BF16), kd)

    def body(ci, sts):
        return tuple(one_head(ci, st, hh) for hh, st in enumerate(sts))

    sts = lax.fori_loop(0, seq // c, body, (jnp.zeros((DV_B, dk), F32),) * heads)
    for hh, st in enumerate(sts):
        s_ref[0, hh] = jnp.transpose(st)


def _hgrn_prompt(hg, lb, gnorm_w, batch, seq, chunk, heads):
    ng = H_B // heads
    blk = lambda j: pl.BlockSpec((seq, heads * DK_B), lambda b, g: (b, j * ng + g))
    return pl.pallas_call(
        functools.partial(_hgrn_prompt_kernel, chunk=chunk),
        grid=(batch, ng),
        in_specs=[blk(0), blk(1), blk(2), blk(3),
                  pl.BlockSpec((heads, 1, DK_B), lambda b, g: (g, 0, 0)),
                  pl.BlockSpec((1, DV_B), lambda b, g: (0, 0))],
        out_specs=[pl.BlockSpec((seq, heads * DV_B), lambda b, g: (b, g)),
                   pl.BlockSpec((1, heads, DK_B, DV_B), lambda b, g: (b, g, 0, 0))],
        out_shape=[jax.ShapeDtypeStruct((batch * seq, H_B * DV_B), F32),
                   jax.ShapeDtypeStruct((batch, H_B, DK_B, DV_B), F32)],
        compiler_params=_cparams(("parallel", "parallel")),
        name="hgrn_prompt",
    )(hg, hg, hg, hg, lb, gnorm_w)


def _hgrn_sample_kernel(hg_ref, lb_ref, w_ref, s0_ref, o_ref, s_ref):
    nb = s0_ref.shape[0]
    for b in range(nb):
        for h in range(H_B):
            sl = lambda j: slice((j * H_B + h) * DK_B, (j * H_B + h + 1) * DK_B)
            row = lambda j: hg_ref[b, :, sl(j)]
            q, k, lf = _hgrn_gates(row(0), row(1), lb_ref[h])
            v = row(2)
            stack = jnp.concatenate(
                [q, k, jnp.exp(lf), jnp.zeros((DK_B - 3, DK_B), F32)], axis=0)
            cols = jnp.transpose(stack)
            s_new = cols[:, 2:3] * s0_ref[b, h] + cols[:, 1:2] * v
            s_ref[b, h] = s_new
            o = jnp.sum(cols[:, 0:1] * s_new, axis=0, keepdims=True)
            o_ref[b, :, h * DV_B:(h + 1) * DV_B] = _hgrn_post(o, row(3), w_ref[...])


def _hgrn_sample(hg, lb, gnorm_w, state, tb):
    nb = state.shape[0]
    return pl.pallas_call(
        _hgrn_sample_kernel,
        grid=(nb // tb,),
        in_specs=[pl.BlockSpec((tb, 1, N_HG), lambda i: (i, 0, 0)),
                  pl.BlockSpec((H_B, 1, DK_B), lambda i: (0, 0, 0)),
                  pl.BlockSpec((1, DV_B), lambda i: (0, 0)),
                  pl.BlockSpec((tb, H_B, DK_B, DV_B), lambda i: (i, 0, 0, 0))],
        out_specs=[pl.BlockSpec((tb, 1, H_B * DV_B), lambda i: (i, 0, 0)),
                   pl.BlockSpec((tb, H_B, DK_B, DV_B), lambda i: (i, 0, 0, 0))],
        out_shape=[jax.ShapeDtypeStruct((nb, 1, H_B * DV_B), F32),
                   jax.ShapeDtypeStruct(state.shape, F32)],
        compiler_params=_cparams(("parallel",)),
        name="hgrn_sample",
    )(hg.reshape(nb, 1, N_HG), lb, gnorm_w, state)


def _merge_kernel(x_ref, oa_ref, ob_ref, gt_ref, wa_ref, wb_ref, wo_ref, n2w_ref, wq_ref,
                  keys_ref, h_ref, nt_ref, s1_ref, s2_ref):
    ga = gt_ref[:, :D_MODEL]
    gb = gt_ref[:, D_MODEL:]
    m = (_sigmoid(ga) * _dot(oa_ref[...].astype(BF16), wa_ref[...])
         + _sigmoid(gb) * _dot(ob_ref[...].astype(BF16), wb_ref[...]))
    h = x_ref[...] + _dot(m.astype(BF16), wo_ref[...])
    h_ref[...] = h
    n2 = _rms(h, n2w_ref[...])
    nt_ref[...] = jnp.transpose(n2).astype(BF16)
    qp = _dot(n2.astype(BF16), wq_ref[...]).astype(BF16)
    s1_ref[...] = _dot_nt(keys_ref[0], qp)
    s2_ref[...] = _dot_nt(keys_ref[1], qp)


def _merge(x, oa, ob, gt, wa, wb, wo, n2w, wq, keys, tm):
    t = x.shape[0]
    row = lambda n: pl.BlockSpec((tm, n), lambda i: (i, 0))
    full = lambda a: pl.BlockSpec(a.shape, lambda i: (0,) * a.ndim)
    sspec = pl.BlockSpec((N_KEYS * H_P, tm), lambda i: (0, i))
    return pl.pallas_call(
        _merge_kernel,
        grid=(t // tm,),
        in_specs=[row(D_MODEL), row(N_QK), row(H_B * DV_B), row(N_GT),
                  full(wa), full(wb), full(wo), full(n2w), full(wq), full(keys)],
        out_specs=[row(D_MODEL), pl.BlockSpec((D_MODEL, tm), lambda i: (0, i)), sspec, sspec],
        out_shape=[jax.ShapeDtypeStruct((t, D_MODEL), F32),
                   jax.ShapeDtypeStruct((D_MODEL, t), BF16),
                   jax.ShapeDtypeStruct((N_KEYS * H_P, t), F32),
                   jax.ShapeDtypeStruct((N_KEYS * H_P, t), F32)],
        compiler_params=_cparams(("parallel",)),
        name="merge",
    )(x, oa, ob, gt, wa, wb, wo, n2w, wq, keys)


def _bitonic_merge_desc(a):
    a = list(a)
    d = len(a) // 2
    while d >= 1:
        for i in range(len(a)):
            if i & d == 0:
                a[i], a[i + d] = jnp.maximum(a[i], a[i + d]), jnp.minimum(a[i], a[i + d])
        d //= 2
    return a


def _sort_desc(a):
    if len(a) == 1:
        return list(a)
    half = len(a) // 2
    return _bitonic_merge_desc(_sort_desc(a[:half]) + _sort_desc(a[half:])[::-1])


def _top16(vals):
    groups = [_sort_desc(vals[i:i + TOPK_P]) for i in range(0, len(vals), TOPK_P)]
    while len(groups) > 1:
        groups = [_bitonic_merge_desc([jnp.maximum(a[i], b[TOPK_P - 1 - i])
                                       for i in range(TOPK_P)])
                  for a, b in zip(groups[0::2], groups[1::2])]
    return groups[0]


def _topk_kernel(s1_ref, s2_ref, cnt_ref, e1_ref, r2_ref, e2_ref):
    tl = s1_ref.shape[1]
    key_rows = lambda ref, e: ref[pl.ds(e * H_P, H_P), :]
    t1 = _top16([key_rows(s1_ref, e) for e in range(N_KEYS)])
    t2 = _top16([key_rows(s2_ref, e) for e in range(N_KEYS)])
    pairs = [(p, r) for p in range(TOPK_P) for r in range(TOPK_P) if (p + 1) * (r + 1) <= TOPK_P]
    sums = {pr: t1[pr[0]] + t2[pr[1]] for pr in pairs}
    pad = [jnp.full((H_P, tl), NEG, F32)] * (-len(pairs) % (2 * TOPK_P))
    top = _top16([sums[pr] for pr in pairs] + pad)
    tau = top[TOPK_P - 1]
    z = functools.reduce(jnp.add, [jnp.exp(c - top[0]) for c in top])
    theta = []
    for r in range(TOPK_P):
        th = jnp.full((H_P, tl), -NEG, F32)
        for p in range(TOPK_P // (r + 1)):
            th = jnp.where(sums[(p, r)] >= tau, t1[p], th)
        theta.append(th)
    shift1 = t1[0] + jnp.log(z)
    for e in range(N_KEYS):
        x = key_rows(s1_ref, e)
        cnt = jnp.zeros((H_P, tl), F32)
        for r in range(TOPK_P):
            cnt = cnt + jnp.where(x >= theta[r], 1.0, 0.0)
        cnt_ref[pl.ds(e * H_P, H_P), :] = cnt
        e1_ref[pl.ds(e * H_P, H_P), :] = jnp.exp(x - shift1)
    for h in range(H_P):
        xh = s2_ref[pl.ds(h, N_KEYS, stride=H_P), :]
        rank = jnp.zeros((N_KEYS, tl), F32)
        for r in range(TOPK_P):
            rank = rank + jnp.where(xh < t2[r][h:h + 1, :], 1.0, 0.0)
        r2_ref[h] = rank.astype(BF16)
        e2_ref[h] = jnp.exp(xh - t2[0][h:h + 1, :]).astype(BF16)


def _topk(s1, s2, tl):
    t = s1.shape[1]
    sspec = pl.BlockSpec((N_KEYS * H_P, tl), lambda i: (0, i))
    hspec = pl.BlockSpec((H_P, N_KEYS, tl), lambda i: (0, 0, i))
    return pl.pallas_call(
        _topk_kernel,
        grid=(t // tl,),
        in_specs=[sspec, sspec],
        out_specs=[sspec, sspec, hspec, hspec],
        out_shape=[jax.ShapeDtypeStruct((N_KEYS * H_P, t), F32),
                   jax.ShapeDtypeStruct((N_KEYS * H_P, t), F32),
                   jax.ShapeDtypeStruct((H_P, N_KEYS, t), BF16),
                   jax.ShapeDtypeStruct((H_P, N_KEYS, t), BF16)],
        compiler_params=_cparams(("parallel",)),
        name="topk",
    )(s1, s2)


def _gelu(x):
    return 0.5 * x * (1.0 + lax.erf(x * (2.0 ** -0.5)))


def _peer_kernel(xt_ref, u_ref, vt_ref, cnt_ref, e1_ref, r2_ref, e2_ref, h_ref, nfw_ref,
                 y_ref, acc_ref, *, e1_per_step):
    j = pl.program_id(1)

    @pl.when(j == 0)
    def _():
        acc_ref[...] = jnp.zeros(acc_ref.shape, F32)

    def weights(e1):
        w = None
        for hh in range(H_P):
            row = pl.ds(e1 * H_P + hh, 1)
            cnt = cnt_ref[row, :].astype(BF16)
            e2 = e2_ref[hh]
            term = jnp.where(r2_ref[hh] < cnt, e2, jnp.zeros_like(e2)) * e1_ref[row, :].astype(BF16)
            w = term if w is None else w + term
        return w

    w = jnp.concatenate([weights(j * e1_per_step + a) for a in range(e1_per_step)], axis=0)
    act = _gelu(_dot(u_ref[...], xt_ref[...]))
    acc_ref[...] += _dot(vt_ref[...], w * act.astype(BF16))

    @pl.when(j == pl.num_programs(1) - 1)
    def _():
        y_ref[...] = _rms(h_ref[...] + jnp.transpose(acc_ref[...]), nfw_ref[...])


def _peer(xt, u, v, cnt, e1w, r2, e2w, h, nfw, tm, e1_per_step):
    t = h.shape[0]
    te = e1_per_step * N_KEYS
    kspec = pl.BlockSpec((N_KEYS * H_P, tm), lambda i, j: (0, i))
    hspec = pl.BlockSpec((H_P, N_KEYS, tm), lambda i, j: (0, 0, i))
    row = pl.BlockSpec((tm, D_MODEL), lambda i, j: (i, 0))
    return pl.pallas_call(
        functools.partial(_peer_kernel, e1_per_step=e1_per_step),
        grid=(t // tm, N_KEYS // e1_per_step),
        in_specs=[pl.BlockSpec((D_MODEL, tm), lambda i, j: (0, i)),
                  pl.BlockSpec((te, D_MODEL), lambda i, j: (j, 0)),
                  pl.BlockSpec((D_MODEL, te), lambda i, j: (0, j)),
                  kspec, kspec, hspec, hspec, row,
                  pl.BlockSpec((1, D_MODEL), lambda i, j: (0, 0))],
        out_specs=row,
        out_shape=jax.ShapeDtypeStruct((t, D_MODEL), F32),
        scratch_shapes=[pltpu.VMEM((D_MODEL, tm), F32)],
        compiler_params=_cparams(("parallel", "arbitrary")),
        name="peer",
    )(xt, u, v, cnt, e1w, r2, e2w, h, nfw)


def _tile(n, pref):
    t = min(n, pref)
    assert n % t == 0, (n, t)
    return t


def kernel(x_prompt, x_sample, cache_k, cache_v, state_hgrn, page_table, norm1_w, w_in,
           lambda_q1, lambda_k1, lambda_q2, lambda_k2, subln_w, lb_param, gnorm_w,
           w_branch_a, w_branch_b, w_out, norm2_w, w_query, sub_keys, expert_u, expert_v,
           norm_f_w):
    depth = w_in.shape[0]
    assert depth == 1 and w_in.shape[2] == N_IN
    batch, seq, _ = x_prompt.shape
    n_dec, dec_len, _ = x_sample.shape
    assert dec_len == 1
    l = 0

    lam_init = 0.8 - 0.6 * math.exp(-0.3 * l)
    post_scale = 1.0 - lam_init
    f = F32
    lam = (jnp.exp(jnp.sum(lambda_q1[l].astype(f) * lambda_k1[l].astype(f)))
           - jnp.exp(jnp.sum(lambda_q2[l].astype(f) * lambda_k2[l].astype(f)))
           + lam_init).reshape(1)
    lb = jnp.cumsum(jax.nn.softmax(lb_param.astype(f), axis=0), axis=0)[l].reshape(H_B, 1, DK_B)

    n1w = norm1_w[l].reshape(1, D_MODEL)
    n2w = norm2_w[l].reshape(1, D_MODEL)
    nfw = norm_f_w.reshape(1, D_MODEL)
    sub_w = subln_w[l].reshape(1, DV_A)
    gn_w = gnorm_w[l].reshape(1, DV_B)
    win = w_in[l].astype(BF16)
    wa = w_branch_a[l].astype(BF16)
    wb = w_branch_b[l].astype(BF16)
    wo = w_out[l].astype(BF16)
    wq = w_query[l].astype(BF16)
    sk = jnp.transpose(sub_keys[l].astype(BF16), (1, 2, 0, 3))
    eye_h = jnp.eye(H_P, dtype=BF16)
    eye_c = jnp.eye(2, dtype=BF16)
    keys = jnp.einsum('cnhd,hg,cb->cnhgbd', sk, eye_h, eye_c).reshape(
        2, N_KEYS * H_P, H_P * 2 * DK_P_HALF)
    u = expert_u.reshape(expert_u.shape[1:]).astype(BF16)
    v = jnp.transpose(expert_v.reshape(expert_v.shape[1:])).astype(BF16)

    def tail(x2, oa, ob, gt, tm_merge, tl, tm_peer, e1_per_step=4):
        h, nt, s1, s2 = _merge(x2, oa, ob, gt, wa, wb, wo, n2w, wq, keys, tm_merge)
        cnt, e1w, r2, e2w = _topk(s1, s2, tl)
        return _peer(nt, u, v, cnt, e1w, r2, e2w, h, nfw, tm_peer, e1_per_step)

    wkvt = jnp.transpose(w_in[l, :, N_QK:3 * N_QK]).astype(BF16)

    def keys_out(kt, b, s):
        return jnp.transpose(kt.reshape(1, b, H_A, 2, DH_QK, s), (0, 1, 5, 2, 3, 4))

    t = batch * seq
    xp = x_prompt.reshape(t, D_MODEL)
    qa, ka, kt, va, vat, hg, gt = _in_proj(xp, n1w, win, wkvt, batch, _tile(seq, 256))
    tq = _tile(seq, 1024)
    oa = _attn_prompt(lam, qa, ka, vat, sub_w, batch, seq, tq, _tile(tq, 1024), post_scale)
    ob, s_p = _hgrn_prompt(hg, lb, gn_w, batch, seq, _tile(seq, 128), 2)
    y_p = tail(xp, oa, ob, gt, _tile(t, 256), _tile(t, 128), _tile(t, 512))

    xs = x_sample.reshape(n_dec, D_MODEL)
    qa_s, ka_s, kt_s, va_s, _, hg_s, gt_s = _in_proj(xs, n1w, win, wkvt, 1, _tile(n_dec, 128))
    n_phys, page = cache_k.shape[1], cache_k.shape[2]
    cache_kt = jnp.transpose(cache_k, (0, 1, 3, 4, 5, 2)).reshape(n_phys, N_QK, page)
    cache_v2 = cache_v.reshape(n_phys, page * H_A, DV_A)
    oa_s = _attn_decode(page_table, lam, qa_s, ka_s, va_s, sub_w, cache_kt, cache_v2, post_scale)
    ob_s, s_s = _hgrn_sample(hg_s, lb, gn_w, state_hgrn.reshape(state_hgrn.shape[1:]),
                             _tile(n_dec, 8))
    y_s = tail(xs, oa_s, ob_s.reshape(n_dec, H_B * DV_B), gt_s,
               _tile(n_dec, 128), _tile(n_dec, 128), _tile(n_dec, 128))

    return (y_p.reshape(batch, seq, D_MODEL),
            y_s.reshape(n_dec, 1, D_MODEL),
            keys_out(kt, batch, seq),
            va.reshape(1, batch, seq, H_A, DV_A),
            s_p.reshape(1, batch, H_B, DK_B, DV_B),
            jnp.transpose(keys_out(kt_s, 1, n_dec), (0, 2, 1, 3, 4, 5)),
            va_s.reshape(1, n_dec, 1, H_A, DV_A),
            s_s.reshape(1, n_dec, H_B, DK_B, DV_B))
```

```python
import functools
import math

import jax
import jax.numpy as jnp
from jax import lax
from jax.experimental import pallas as pl
from jax.experimental.pallas import tpu as pltpu

F32 = jnp.float32
BF16 = jnp.bfloat16

EPS = 1e-6
NEG = -1e30
LOG2E = math.log2(math.e)
LANES = 128
SUBLANES = 8
VMEM_LIMIT = 56 * 1024 * 1024

H_A = 4
DH_QK = 64
DV_A = 128
H_B = 4
DK_B = 128
DV_B = 128
N_KEYS = 128
H_P = 8
DK_P_HALF = 64
TOPK_P = 16
D_MODEL = 1024
N_QK = H_A * 2 * DH_QK
N_HG = 4 * H_B * DK_B
N_GT = 2 * D_MODEL
N_IN = 3 * N_QK + N_HG + N_GT


def _cparams(sem):
    return pltpu.CompilerParams(dimension_semantics=sem, vmem_limit_bytes=VMEM_LIMIT)


def _rms(x, w):
    return x * lax.rsqrt(jnp.mean(x * x, axis=-1, keepdims=True) + EPS) * w


def _sigmoid(x):
    return 1.0 / (1.0 + jnp.exp(-x))


def _silu(x):
    return x * _sigmoid(x)


def _dot(a, b):
    return jnp.dot(a, b, preferred_element_type=F32)


def _dot_nt(a, b):
    return lax.dot_general(a, b, (((1,), (1,)), ((), ())), preferred_element_type=F32)


def _inproj_kernel(x_ref, g_ref, w_ref, qa_ref, ka_ref, kt_ref, va_ref, vt_ref,
                   hg_ref, gt_ref):
    nb = _rms(x_ref[...], g_ref[...]).astype(BF16)
    cw = N_QK

    def mm(c):
        return _dot(nb, w_ref[:, c * cw:(c + 1) * cw])

    qa_ref[...] = (mm(0) * (DH_QK ** -0.5 * LOG2E)).astype(BF16)
    ka = mm(1)
    ka_ref[...] = ka.astype(BF16)
    kt_ref[0] = jnp.transpose(ka)
    va = mm(2)
    va_ref[...] = va
    vt_ref[0] = jnp.transpose(va).astype(BF16)
    for c in range(N_HG // cw):
        hg_ref[:, c * cw:(c + 1) * cw] = mm(3 + c)
    for c in range(N_GT // cw):
        gt_ref[:, c * cw:(c + 1) * cw] = mm(3 + N_HG // cw + c)


def _in_proj(x, g, w, batch, tm):
    m = x.shape[0]
    seq = m // batch
    nt = seq // tm
    row = lambda n: pl.BlockSpec((tm, n), lambda b, i: (b * nt + i, 0))
    full = lambda a: pl.BlockSpec(a.shape, lambda b, i: (0, 0))
    tspec = pl.BlockSpec((1, N_QK, tm), lambda b, i: (b, 0, i))
    return pl.pallas_call(
        _inproj_kernel,
        grid=(batch, nt),
        in_specs=[row(D_MODEL), full(g), full(w)],
        out_specs=[row(N_QK), row(N_QK), tspec, row(N_QK), tspec, row(N_HG), row(N_GT)],
        out_shape=[jax.ShapeDtypeStruct((m, N_QK), BF16),
                   jax.ShapeDtypeStruct((m, N_QK), BF16),
                   jax.ShapeDtypeStruct((batch, N_QK, seq), F32),
                   jax.ShapeDtypeStruct((m, N_QK), F32),
                   jax.ShapeDtypeStruct((batch, N_QK, seq), BF16),
                   jax.ShapeDtypeStruct((m, N_HG), F32),
                   jax.ShapeDtypeStruct((m, N_GT), F32)],
        compiler_params=_cparams(("parallel", "parallel")),
        name="in_proj",
    )(x, g, w)


def _attn_kernel(lam_ref, q_ref, k_ref, vt_ref, w_ref, o_ref,
                 m1, l1, a1, m2, l2, a2, *, tk, post_scale):
    qi = pl.program_id(2)
    tq = q_ref.shape[0]

    for m, l, a in ((m1, l1, a1), (m2, l2, a2)):
        m[...] = jnp.full(m.shape, NEG, F32)
        l[...] = jnp.zeros(l.shape, F32)
        a[...] = jnp.zeros(a.shape, F32)

    q = q_ref[...]
    lane = lax.broadcasted_iota(jnp.int32, q.shape, 1)
    qcs = [jnp.where((lane >= DH_QK) == bool(c), q, jnp.zeros_like(q)) for c in range(2)]

    def step(start, diag_offset):
        start = pl.multiple_of(start, tk)
        k = k_ref[pl.ds(start, tk), :]
        vt = vt_ref[0, :, pl.ds(start, tk)]
        if diag_offset is not None:
            krow = lax.broadcasted_iota(jnp.int32, (tk, tq), 0) + diag_offset
            qcol = lax.broadcasted_iota(jnp.int32, (tk, tq), 1)
            keep = krow <= qcol
        for qc, (m, l, a) in zip(qcs, ((m1, l1, a1), (m2, l2, a2))):
            s = _dot_nt(k, qc)
            if diag_offset is not None:
                s = jnp.where(keep, s, NEG)
            m_prev = m[...]
            m_new = jnp.maximum(m_prev, jnp.max(s, axis=0, keepdims=True))
            alpha = jnp.exp2(m_prev - m_new)
            p = jnp.exp2(s - m_new)
            l[...] = alpha * l[...] + jnp.sum(p, axis=0, keepdims=True)
            a[...] = alpha * a[...] + _dot(vt, p.astype(BF16))
            m[...] = m_new

    def body(ki, carry):
        step(ki * tk, None)
        return carry

    lax.fori_loop(0, qi * (tq // tk), body, 0)
    for j in range(tq // tk):
        step(qi * tq + j * tk, j * tk)

    lam = lam_ref[0]
    ot = a1[...] / l1[...] - lam * (a2[...] / l2[...])
    o_ref[...] = _rms(jnp.transpose(ot), w_ref[...]) * post_scale


def _attn_prompt(lam, qa, ka, vat, subln_w, batch, seq, tq, tk, post_scale):
    nq = seq // tq
    return pl.pallas_call(
        functools.partial(_attn_kernel, tk=tk, post_scale=post_scale),
        grid=(batch, H_A, nq),
        in_specs=[pl.BlockSpec(memory_space=pltpu.SMEM),
                  pl.BlockSpec((tq, DV_A), lambda b, h, qi: (b * nq + qi, h)),
                  pl.BlockSpec((seq, DV_A), lambda b, h, qi: (b, h)),
                  pl.BlockSpec((1, DV_A, seq), lambda b, h, qi: (b, h, 0)),
                  pl.BlockSpec((1, DV_A), lambda b, h, qi: (0, 0))],
        out_specs=pl.BlockSpec((tq, DV_A), lambda b, h, qi: (b * nq + qi, h)),
        out_shape=jax.ShapeDtypeStruct((batch * seq, H_A * DV_A), F32),
        scratch_shapes=[pltpu.VMEM((1, tq), F32), pltpu.VMEM((1, tq), F32),
                        pltpu.VMEM((DV_A, tq), F32)] * 2,
        compiler_params=_cparams(("parallel", "parallel", "arbitrary")),
        name="attn_prompt",
    )(lam, qa, ka, vat, subln_w)


def _decode_kernel(pt_ref, lam_ref, q_ref, kn_ref, vn_ref, w_ref, *refs, n_pages, post_scale):
    k_refs = refs[:n_pages]
    v_refs = refs[n_pages:2 * n_pages]
    o_ref = refs[2 * n_pages]
    del pt_ref
    nmap = 2 * H_A
    width = N_QK
    page = k_refs[0].shape[2]

    q = q_ref[0].astype(F32)
    rows = lax.broadcasted_iota(jnp.int32, (nmap, width), 0)
    lane = lax.broadcasted_iota(jnp.int32, (nmap, width), 1)
    qrows = jnp.where(lane // DH_QK == rows, jnp.broadcast_to(q, (nmap, width)), 0.0)
    qb = qrows.astype(BF16)

    s_new = jnp.sum(qrows * kn_ref[0].astype(F32), axis=-1, keepdims=True)
    s = [_dot(qb, k_refs[i][0].astype(BF16)) for i in range(n_pages)]
    m = functools.reduce(jnp.maximum, s)
    m = jnp.maximum(jnp.max(m, axis=-1, keepdims=True), s_new)
    p = [jnp.exp2(si - m) for si in s]
    p_new = jnp.exp2(s_new - m)
    l = jnp.sum(functools.reduce(jnp.add, p), axis=-1, keepdims=True) + p_new

    r8 = lax.broadcasted_iota(jnp.int32, (nmap, page), 0)
    o8 = jnp.zeros((nmap, DV_A), F32)
    for i in range(n_pages):
        pb = p[i].astype(BF16)
        for h in range(H_A):
            vh = v_refs[i][0, pl.ds(h, page, stride=H_A), :].astype(BF16)
            o8 = o8 + _dot(jnp.where(r8 // 2 == h, pb, jnp.zeros_like(pb)), vh)

    vn = vn_ref[0].astype(BF16).astype(F32)
    r8v = lax.broadcasted_iota(jnp.int32, (nmap, DV_A), 0)
    vn8 = jnp.zeros((nmap, DV_A), F32)
    for h in range(H_A):
        vn8 = jnp.where(r8v // 2 == h, jnp.broadcast_to(vn[:, h * DV_A:(h + 1) * DV_A],
                                                        (nmap, DV_A)), vn8)
    o8 = (o8 + p_new.astype(BF16).astype(F32) * vn8) / l
    lam = lam_ref[0]
    w = w_ref[...]
    for h in range(H_A):
        d = o8[2 * h:2 * h + 1] - lam * o8[2 * h + 1:2 * h + 2]
        o_ref[0, :, h * DV_A:(h + 1) * DV_A] = _rms(d, w) * post_scale


def _attn_decode(page_table, lam, qa, ka, va, subln_w, cache_kt, cache_v, post_scale):
    nb, n_pages = page_table.shape
    width, page = cache_kt.shape[1], cache_kt.shape[2]
    one = lambda: pl.BlockSpec((1, 1, width), lambda b, pt: (b, 0, 0))
    kspec = lambda i: pl.BlockSpec((1, width, page), lambda b, pt: (pt[b, i], 0, 0))
    vspec = lambda i: pl.BlockSpec((1, page * H_A, DV_A), lambda b, pt: (pt[b, i], 0, 0))
    grid_spec = pltpu.PrefetchScalarGridSpec(
        num_scalar_prefetch=1,
        grid=(nb,),
        in_specs=[pl.BlockSpec(memory_space=pltpu.SMEM), one(), one(), one(),
                  pl.BlockSpec((1, DV_A), lambda b, pt: (0, 0))]
                 + [kspec(i) for i in range(n_pages)] + [vspec(i) for i in range(n_pages)],
        out_specs=one(),
    )
    r3 = lambda a: a.reshape(nb, 1, width)
    out = pl.pallas_call(
        functools.partial(_decode_kernel, n_pages=n_pages, post_scale=post_scale),
        grid_spec=grid_spec,
        out_shape=jax.ShapeDtypeStruct((nb, 1, width), F32),
        compiler_params=_cparams(("arbitrary",)),
        name="attn_decode",
    )(page_table, lam, r3(qa), r3(ka), r3(va), subln_w,
      *([cache_kt] * n_pages), *([cache_v] * n_pages))
    return out.reshape(nb, width)


def _hgrn_gates(qb, fb, lb):
    q = _silu(qb) * (DK_B ** -0.5)
    f = lb + (1.0 - lb) * _sigmoid(fb)
    return q, 1.0 - f, jnp.log(f)


def _hgrn_post(o, gb, w):
    return _rms(o, w) * _silu(gb)


def _hgrn_prompt_kernel(qb_ref, fb_ref, ib_ref, gb_ref, lb_ref, w_ref, o_ref, s_ref, *, chunk):
    c = chunk
    dk = DK_B
    seq = qb_ref.shape[0]
    heads = lb_ref.shape[0]
    gw = w_ref[...]

    row = lax.broadcasted_iota(jnp.int32, (c, c), 0)
    col = lax.broadcasted_iota(jnp.int32, (c, c), 1)
    tril = jnp.where(col <= row, 1.0, 0.0).astype(BF16)
    ones = jnp.ones((dk, c), BF16)
    levels = []
    m = SUBLANES
    while 2 * m <= c:
        levels.append((m, jnp.where(row // (2 * m) == col // (2 * m), 1.0, 0.0)))
        m *= 2
    diag = [jnp.where(col == (row // SUBLANES) * SUBLANES + s, 1.0, 0.0) for s in range(SUBLANES)]

    def one_head(ci, st, hh):
        rows = pl.ds(pl.multiple_of(ci * c, c), c)
        lanes = slice(hh * dk, (hh + 1) * dk)
        q, k, lf = _hgrn_gates(qb_ref[rows, lanes], fb_ref[rows, lanes], lb_ref[hh])
        v = ib_ref[rows, lanes]

        hi = lf.astype(BF16)
        r1 = lf - hi.astype(F32)
        mid = r1.astype(BF16)
        lo = (r1 - mid.astype(F32)).astype(BF16)
        g = _dot(tril, hi) + _dot(tril, mid) + _dot(tril, lo)

        o = _dot_nt((q * jnp.exp(g)).astype(BF16), st.astype(BF16))

        a = jnp.zeros((c, c), F32)
        for m, same_block in levels:
            nb = c // (2 * m)
            g3 = g.reshape(nb, 2 * m, dk)
            d = g3 - g3[:, m - 1:m, :]
            rin = lax.broadcasted_iota(jnp.int32, (nb, 2 * m, dk), 1)
            qs = jnp.where(rin >= m, q.reshape(nb, 2 * m, dk) * jnp.exp(jnp.minimum(d, 0.0)), 0.0)
            ks = jnp.where(rin < m, k.reshape(nb, 2 * m, dk) * jnp.exp(jnp.minimum(-d, 0.0)), 0.0)
            al = _dot_nt(qs.reshape(c, dk).astype(BF16), ks.reshape(c, dk).astype(BF16))
            a = a + al * same_block

        nb = c // SUBLANES
        g8 = g.reshape(nb, SUBLANES, dk)
        q8 = q.reshape(nb, SUBLANES, dk)
        k8 = k.reshape(nb, SUBLANES, dk)
        rin = lax.broadcasted_iota(jnp.int32, (nb, SUBLANES, dk), 1)
        for s in range(SUBLANES):
            d = g8 - g8[:, s:s + 1, :]
            p = jnp.where(rin >= s, q8 * k8[:, s:s + 1, :] * jnp.exp(jnp.minimum(d, 0.0)), 0.0)
            r = _dot(p.reshape(c, dk).astype(BF16), ones)
            a = a + r * diag[s]

        o = o + _dot(a.astype(BF16), v.astype(BF16))
        o_ref[rows, lanes] = _hgrn_post(o, gb_ref[rows, lanes], gw)

        g_end = g[c - 1:c, :]
        kd = (k * jnp.exp(g_end - g)).astype(BF16)
        return st * jnp.exp(g_end) + _dot(jnp.transpose(v).astype(BF16), kd)

    def body(ci, sts):
        return tuple(one_head(ci, st, hh) for hh, st in enumerate(sts))

    sts = lax.fori_loop(0, seq // c, body, (jnp.zeros((DV_B, dk), F32),) * heads)
    for hh, st in enumerate(sts):
        s_ref[0, hh] = jnp.transpose(st)


def _hgrn_prompt(hg, lb, gnorm_w, batch, seq, chunk, heads):
    ng = H_B // heads
    blk = lambda j: pl.BlockSpec((seq, heads * DK_B), lambda b, g: (b, j * ng + g))
    return pl.pallas_call(
        functools.partial(_hgrn_prompt_kernel, chunk=chunk),
        grid=(batch, ng),
        in_specs=[blk(0), blk(1), blk(2), blk(3),
                  pl.BlockSpec((heads, 1, DK_B), lambda b, g: (g, 0, 0)),
                  pl.BlockSpec((1, DV_B), lambda b, g: (0, 0))],
        out_specs=[pl.BlockSpec((seq, heads * DV_B), lambda b, g: (b, g)),
                   pl.BlockSpec((1, heads, DK_B, DV_B), lambda b, g: (b, g, 0, 0))],
        out_shape=[jax.ShapeDtypeStruct((batch * seq, H_B * DV_B), F32),
                   jax.ShapeDtypeStruct((batch, H_B, DK_B, DV_B), F32)],
        compiler_params=_cparams(("parallel", "parallel")),
        name="hgrn_prompt",
    )(hg, hg, hg, hg, lb, gnorm_w)


def _hgrn_sample_kernel(hg_ref, lb_ref, w_ref, s0_ref, o_ref, s_ref):
    nb = s0_ref.shape[0]
    for b in range(nb):
        for h in range(H_B):
            sl = lambda j: slice((j * H_B + h) * DK_B, (j * H_B + h + 1) * DK_B)
            row = lambda j: hg_ref[b, :, sl(j)]
            q, k, lf = _hgrn_gates(row(0), row(1), lb_ref[h])
            v = row(2)
            stack = jnp.concatenate(
                [q, k, jnp.exp(lf), jnp.zeros((DK_B - 3, DK_B), F32)], axis=0)
            cols = jnp.transpose(stack)
            s_new = cols[:, 2:3] * s0_ref[b, h] + cols[:, 1:2] * v
            s_ref[b, h] = s_new
            o = jnp.sum(cols[:, 0:1] * s_new, axis=0, keepdims=True)
            o_ref[b, :, h * DV_B:(h + 1) * DV_B] = _hgrn_post(o, row(3), w_ref[...])


def _hgrn_sample(hg, lb, gnorm_w, state, tb):
    nb = state.shape[0]
    return pl.pallas_call(
        _hgrn_sample_kernel,
        grid=(nb // tb,),
        in_specs=[pl.BlockSpec((tb, 1, N_HG), lambda i: (i, 0, 0)),
                  pl.BlockSpec((H_B, 1, DK_B), lambda i: (0, 0, 0)),
                  pl.BlockSpec((1, DV_B), lambda i: (0, 0)),
                  pl.BlockSpec((tb, H_B, DK_B, DV_B), lambda i: (i, 0, 0, 0))],
        out_specs=[pl.BlockSpec((tb, 1, H_B * DV_B), lambda i: (i, 0, 0)),
                   pl.BlockSpec((tb, H_B, DK_B, DV_B), lambda i: (i, 0, 0, 0))],
        out_shape=[jax.ShapeDtypeStruct((nb, 1, H_B * DV_B), F32),
                   jax.ShapeDtypeStruct(state.shape, F32)],
        compiler_params=_cparams(("parallel",)),
        name="hgrn_sample",
    )(hg.reshape(nb, 1, N_HG), lb, gnorm_w, state)


def _merge_kernel(x_ref, oa_ref, ob_ref, gt_ref, wa_ref, wb_ref, wo_ref, n2w_ref, wq_ref,
                  keys_ref, h_ref, nt_ref, s1_ref, s2_ref):
    ga = gt_ref[:, :D_MODEL]
    gb = gt_ref[:, D_MODEL:]
    m = (_sigmoid(ga) * _dot(oa_ref[...].astype(BF16), wa_ref[...])
         + _sigmoid(gb) * _dot(ob_ref[...].astype(BF16), wb_ref[...]))
    h = x_ref[...] + _dot(m.astype(BF16), wo_ref[...])
    h_ref[...] = h
    n2 = _rms(h, n2w_ref[...])
    nt_ref[...] = jnp.transpose(n2).astype(BF16)
    qp = _dot(n2.astype(BF16), wq_ref[...]).astype(BF16)
    for hh in range(H_P):
        qh = qp[:, hh * LANES:(hh + 1) * LANES]
        rows = pl.ds(hh, N_KEYS, stride=H_P)
        for c, s_ref in enumerate((s1_ref, s2_ref)):
            s = _dot_nt(keys_ref[c, hh], qh)
            for blk in range(s_ref.shape[0]):
                s_ref[blk, rows, :] = s[:, blk * LANES:(blk + 1) * LANES]


def _merge(x, oa, ob, gt, wa, wb, wo, n2w, wq, keys, tm):
    t = x.shape[0]
    row = lambda n: pl.BlockSpec((tm, n), lambda i: (i, 0))
    full = lambda a: pl.BlockSpec(a.shape, lambda i: (0,) * a.ndim)
    sspec = pl.BlockSpec((tm // LANES, N_KEYS * H_P, LANES), lambda i: (i, 0, 0))
    return pl.pallas_call(
        _merge_kernel,
        grid=(t // tm,),
        in_specs=[row(D_MODEL), row(N_QK), row(H_B * DV_B), row(N_GT),
                  full(wa), full(wb), full(wo), full(n2w), full(wq), full(keys)],
        out_specs=[row(D_MODEL), pl.BlockSpec((D_MODEL, tm), lambda i: (0, i)), sspec, sspec],
        out_shape=[jax.ShapeDtypeStruct((t, D_MODEL), F32),
                   jax.ShapeDtypeStruct((D_MODEL, t), BF16),
                   jax.ShapeDtypeStruct((t // LANES, N_KEYS * H_P, LANES), F32),
                   jax.ShapeDtypeStruct((t // LANES, N_KEYS * H_P, LANES), F32)],
        compiler_params=_cparams(("parallel",)),
        name="merge",
    )(x, oa, ob, gt, wa, wb, wo, n2w, wq, keys)


def _bitonic_merge_desc(a):
    a = list(a)
    d = len(a) // 2
    while d >= 1:
        for i in range(len(a)):
            if i & d == 0:
                a[i], a[i + d] = jnp.maximum(a[i], a[i + d]), jnp.minimum(a[i], a[i + d])
        d //= 2
    return a


def _sort_desc(a):
    if len(a) == 1:
        return list(a)
    half = len(a) // 2
    return _bitonic_merge_desc(_sort_desc(a[:half]) + _sort_desc(a[half:])[::-1])


def _top16(vals):
    groups = [_sort_desc(vals[i:i + TOPK_P]) for i in range(0, len(vals), TOPK_P)]
    while len(groups) > 1:
        groups = [_bitonic_merge_desc([jnp.maximum(a[i], b[TOPK_P - 1 - i])
                                       for i in range(TOPK_P)])
                  for a, b in zip(groups[0::2], groups[1::2])]
    return groups[0]


def _topk_kernel(s1_ref, s2_ref, cnt_ref, e1_ref, r2_ref, e2_ref):
    tl = s1_ref.shape[2]
    key_rows = lambda ref, e: ref[0, pl.ds(e * H_P, H_P), :]
    t1 = _top16([key_rows(s1_ref, e) for e in range(N_KEYS)])
    t2 = _top16([key_rows(s2_ref, e) for e in range(N_KEYS)])
    pairs = [(p, r) for p in range(TOPK_P) for r in range(TOPK_P) if (p + 1) * (r + 1) <= TOPK_P]
    sums = {pr: t1[pr[0]] + t2[pr[1]] for pr in pairs}
    pad = [jnp.full((H_P, tl), NEG, F32)] * (-len(pairs) % (2 * TOPK_P))
    top = _top16([sums[pr] for pr in pairs] + pad)
    tau = top[TOPK_P - 1]
    z = functools.reduce(jnp.add, [jnp.exp(c - top[0]) for c in top])
    theta = []
    for r in range(TOPK_P):
        th = jnp.full((H_P, tl), -NEG, F32)
        for p in range(TOPK_P // (r + 1)):
            th = jnp.where(sums[(p, r)] >= tau, t1[p], th)
        theta.append(th)
    shift1 = t1[0] + jnp.log(z)
    for e in range(N_KEYS):
        x = key_rows(s1_ref, e)
        cnt = jnp.zeros((H_P, tl), F32)
        for r in range(TOPK_P):
            cnt = jnp.where(x >= theta[r], r + 1.0, cnt)
        cnt_ref[pl.ds(e * H_P, H_P), :] = cnt
        e1_ref[pl.ds(e * H_P, H_P), :] = jnp.exp(x - shift1)
    for h in range(H_P):
        xh = s2_ref[0, pl.ds(h, N_KEYS, stride=H_P), :]
        rank = jnp.zeros((N_KEYS, tl), F32)
        for r in range(TOPK_P):
            rank = jnp.where(xh < t2[r][h:h + 1, :], r + 1.0, rank)
        r2_ref[h] = rank.astype(BF16)
        e2_ref[h] = jnp.exp(xh - t2[0][h:h + 1, :]).astype(BF16)


def _topk(s1, s2):
    tl = s1.shape[2]
    t = s1.shape[0] * tl
    sspec = pl.BlockSpec((N_KEYS * H_P, tl), lambda i: (0, i))
    hspec = pl.BlockSpec((H_P, N_KEYS, tl), lambda i: (0, 0, i))
    ispec = pl.BlockSpec((1, N_KEYS * H_P, tl), lambda i: (i, 0, 0))
    return pl.pallas_call(
        _topk_kernel,
        grid=(t // tl,),
        in_specs=[ispec, ispec],
        out_specs=[sspec, sspec, hspec, hspec],
        out_shape=[jax.ShapeDtypeStruct((N_KEYS * H_P, t), F32),
                   jax.ShapeDtypeStruct((N_KEYS * H_P, t), F32),
                   jax.ShapeDtypeStruct((H_P, N_KEYS, t), BF16),
                   jax.ShapeDtypeStruct((H_P, N_KEYS, t), BF16)],
        compiler_params=_cparams(("parallel",)),
        name="topk",
    )(s1, s2)


def _gelu(x):
    return 0.5 * x * (1.0 + lax.erf(x * (2.0 ** -0.5)))


def _peer_kernel(xt_ref, u_ref, vt_ref, cnt_ref, e1_ref, r2_ref, e2_ref, h_ref, nfw_ref,
                 y_ref, acc_ref, *, e1_per_step):
    j = pl.program_id(1)

    @pl.when(j == 0)
    def _():
        acc_ref[...] = jnp.zeros(acc_ref.shape, F32)

    def weights(e1):
        w = None
        for hh in range(H_P):
            row = pl.ds(e1 * H_P + hh, 1)
            cnt = cnt_ref[row, :].astype(BF16)
            e2 = e2_ref[hh]
            term = jnp.where(r2_ref[hh] < cnt, e2, jnp.zeros_like(e2)) * e1_ref[row, :].astype(BF16)
            w = term if w is None else w + term
        return w

    w = jnp.concatenate([weights(j * e1_per_step + a) for a in range(e1_per_step)], axis=0)
    act = _gelu(_dot(u_ref[...], xt_ref[...]))
    acc_ref[...] += _dot(vt_ref[...], w * act.astype(BF16))

    @pl.when(j == pl.num_programs(1) - 1)
    def _():
        y_ref[...] = _rms(h_ref[...] + jnp.transpose(acc_ref[...]), nfw_ref[...])


def _peer(xt, u, v, cnt, e1w, r2, e2w, h, nfw, tm, e1_per_step):
    t = h.shape[0]
    te = e1_per_step * N_KEYS
    kspec = pl.BlockSpec((N_KEYS * H_P, tm), lambda i, j: (0, i))
    hspec = pl.BlockSpec((H_P, N_KEYS, tm), lambda i, j: (0, 0, i))
    row = pl.BlockSpec((tm, D_MODEL), lambda i, j: (i, 0))
    return pl.pallas_call(
        functools.partial(_peer_kernel, e1_per_step=e1_per_step),
        grid=(t // tm, N_KEYS // e1_per_step),
        in_specs=[pl.BlockSpec((D_MODEL, tm), lambda i, j: (0, i)),
                  pl.BlockSpec((te, D_MODEL), lambda i, j: (j, 0)),
                  pl.BlockSpec((D_MODEL, te), lambda i, j: (0, j)),
                  kspec, kspec, hspec, hspec, row,
                  pl.BlockSpec((1, D_MODEL), lambda i, j: (0, 0))],
        out_specs=row,
        out_shape=jax.ShapeDtypeStruct((t, D_MODEL), F32),
        scratch_shapes=[pltpu.VMEM((D_MODEL, tm), F32)],
        compiler_params=_cparams(("parallel", "arbitrary")),
        name="peer",
    )(xt, u, v, cnt, e1w, r2, e2w, h, nfw)


def _tile(n, pref):
    t = min(n, pref)
    assert n % t == 0, (n, t)
    return t


def kernel(x_prompt, x_sample, cache_k, cache_v, state_hgrn, page_table, norm1_w, w_in,
           lambda_q1, lambda_k1, lambda_q2, lambda_k2, subln_w, lb_param, gnorm_w,
           w_branch_a, w_branch_b, w_out, norm2_w, w_query, sub_keys, expert_u, expert_v,
           norm_f_w):
    depth = w_in.shape[0]
    assert depth == 1 and w_in.shape[2] == N_IN
    batch, seq, _ = x_prompt.shape
    n_dec, dec_len, _ = x_sample.shape
    assert dec_len == 1
    l = 0

    lam_init = 0.8 - 0.6 * math.exp(-0.3 * l)
    post_scale = 1.0 - lam_init
    f = F32
    lam = (jnp.exp(jnp.sum(lambda_q1[l].astype(f) * lambda_k1[l].astype(f)))
           - jnp.exp(jnp.sum(lambda_q2[l].astype(f) * lambda_k2[l].astype(f)))
           + lam_init).reshape(1)
    lb = jnp.cumsum(jax.nn.softmax(lb_param.astype(f), axis=0), axis=0)[l].reshape(H_B, 1, DK_B)

    n1w = norm1_w[l].reshape(1, D_MODEL)
    n2w = norm2_w[l].reshape(1, D_MODEL)
    nfw = norm_f_w.reshape(1, D_MODEL)
    sub_w = subln_w[l].reshape(1, DV_A)
    gn_w = gnorm_w[l].reshape(1, DV_B)
    win = w_in[l].astype(BF16)
    wa = w_branch_a[l].astype(BF16)
    wb = w_branch_b[l].astype(BF16)
    wo = w_out[l].astype(BF16)
    wq = w_query[l].astype(BF16)
    sk = jnp.transpose(sub_keys[l].astype(BF16), (1, 0, 2, 3))
    zk = jnp.zeros_like(sk[0])
    keys = jnp.stack([jnp.concatenate([sk[0], zk], axis=-1),
                      jnp.concatenate([zk, sk[1]], axis=-1)])
    u = expert_u.reshape(expert_u.shape[1:]).astype(BF16)
    v = jnp.transpose(expert_v.reshape(expert_v.shape[1:])).astype(BF16)

    def tail(x2, oa, ob, gt, tm_merge, tm_peer, e1_per_step=16):
        h, nt, s1, s2 = _merge(x2, oa, ob, gt, wa, wb, wo, n2w, wq, keys, tm_merge)
        cnt, e1w, r2, e2w = _topk(s1, s2)
        return _peer(nt, u, v, cnt, e1w, r2, e2w, h, nfw, tm_peer, e1_per_step)


    def keys_out(kt, b, s):
        return jnp.transpose(kt.reshape(1, b, H_A, 2, DH_QK, s), (0, 1, 5, 2, 3, 4))

    t = batch * seq
    xp = x_prompt.reshape(t, D_MODEL)
    qa, ka, kt, va, vat, hg, gt = _in_proj(xp, n1w, win, batch, _tile(seq, 256))
    tq = _tile(seq, 1024)
    oa = _attn_prompt(lam, qa, ka, vat, sub_w, batch, seq, tq, _tile(tq, 1024), post_scale)
    ob, s_p = _hgrn_prompt(hg, lb, gn_w, batch, seq, _tile(seq, 128), 2)
    y_p = tail(xp, oa, ob, gt, _tile(t, 256), _tile(t, 512))

    xs = x_sample.reshape(n_dec, D_MODEL)
    qa_s, ka_s, kt_s, va_s, _, hg_s, gt_s = _in_proj(xs, n1w, win, 1, _tile(n_dec, 128))
    n_phys, page = cache_k.shape[1], cache_k.shape[2]
    cache_kt = jnp.transpose(cache_k, (0, 1, 3, 4, 5, 2)).reshape(n_phys, N_QK, page)
    cache_v2 = cache_v.reshape(n_phys, page * H_A, DV_A)
    oa_s = _attn_decode(page_table, lam, qa_s, ka_s, va_s, sub_w, cache_kt, cache_v2, post_scale)
    ob_s, s_s = _hgrn_sample(hg_s, lb, gn_w, state_hgrn.reshape(state_hgrn.shape[1:]),
                             _tile(n_dec, 8))
    y_s = tail(xs, oa_s, ob_s.reshape(n_dec, H_B * DV_B), gt_s,
               _tile(n_dec, 128), _tile(n_dec, 128))

    return (y_p.reshape(batch, seq, D_MODEL),
            y_s.reshape(n_dec, 1, D_MODEL),
            keys_out(kt, batch, seq),
            va.reshape(1, batch, seq, H_A, DV_A),
            s_p.reshape(1, batch, H_B, DK_B, DV_B),
            jnp.transpose(keys_out(kt_s, 1, n_dec), (0, 2, 1, 3, 4, 5)),
            va_s.reshape(1, n_dec, 1, H_A, DV_A),
            s_s.reshape(1, n_dec, H_B, DK_B, DV_B))
```

```python
import functools
import math

import jax
import jax.numpy as jnp
from jax import lax
from jax.experimental import pallas as pl
from jax.experimental.pallas import tpu as pltpu

F32 = jnp.float32
BF16 = jnp.bfloat16

EPS = 1e-6
NEG = -1e30
LOG2E = math.log2(math.e)
LANES = 128
SUBLANES = 8
VMEM_LIMIT = 56 * 1024 * 1024

H_A = 4
DH_QK = 64
DV_A = 128
H_B = 4
DK_B = 128
DV_B = 128
N_KEYS = 128
H_P = 8
DK_P_HALF = 64
TOPK_P = 16
D_MODEL = 1024
N_QK = H_A * 2 * DH_QK
N_HG = 4 * H_B * DK_B
N_GT = 2 * D_MODEL
N_IN = 3 * N_QK + N_HG + N_GT


def _cparams(sem):
    return pltpu.CompilerParams(dimension_semantics=sem, vmem_limit_bytes=VMEM_LIMIT)


def _rms(x, w):
    return x * lax.rsqrt(jnp.mean(x * x, axis=-1, keepdims=True) + EPS) * w


def _sigmoid(x):
    return 1.0 / (1.0 + jnp.exp(-x))


def _silu(x):
    return x * _sigmoid(x)


def _dot(a, b):
    return jnp.dot(a, b, preferred_element_type=F32)


def _dot_nt(a, b):
    return lax.dot_general(a, b, (((1,), (1,)), ((), ())), preferred_element_type=F32)


def _inproj_kernel(x_ref, g_ref, w_ref, qa_ref, ka_ref, kt_ref, va_ref, vt_ref,
                   hg_ref, gt_ref):
    nb = _rms(x_ref[...], g_ref[...]).astype(BF16)
    cw = N_QK

    def mm(c):
        return _dot(nb, w_ref[:, c * cw:(c + 1) * cw])

    qa_ref[...] = (mm(0) * (DH_QK ** -0.5 * LOG2E)).astype(BF16)
    ka = mm(1)
    ka_ref[...] = ka.astype(BF16)
    kt_ref[0] = jnp.transpose(ka)
    va = mm(2)
    va_ref[...] = va
    vt_ref[0] = jnp.transpose(va).astype(BF16)
    for c in range(N_HG // cw):
        hg_ref[:, c * cw:(c + 1) * cw] = mm(3 + c)
    for c in range(N_GT // cw):
        gt_ref[:, c * cw:(c + 1) * cw] = mm(3 + N_HG // cw + c)


def _in_proj(x, g, w, batch, tm):
    m = x.shape[0]
    seq = m // batch
    nt = seq // tm
    row = lambda n: pl.BlockSpec((tm, n), lambda b, i: (b * nt + i, 0))
    full = lambda a: pl.BlockSpec(a.shape, lambda b, i: (0, 0))
    tspec = pl.BlockSpec((1, N_QK, tm), lambda b, i: (b, 0, i))
    return pl.pallas_call(
        _inproj_kernel,
        grid=(batch, nt),
        in_specs=[row(D_MODEL), full(g), full(w)],
        out_specs=[row(N_QK), row(N_QK), tspec, row(N_QK), tspec, row(N_HG), row(N_GT)],
        out_shape=[jax.ShapeDtypeStruct((m, N_QK), BF16),
                   jax.ShapeDtypeStruct((m, N_QK), BF16),
                   jax.ShapeDtypeStruct((batch, N_QK, seq), F32),
                   jax.ShapeDtypeStruct((m, N_QK), F32),
                   jax.ShapeDtypeStruct((batch, N_QK, seq), BF16),
                   jax.ShapeDtypeStruct((m, N_HG), F32),
                   jax.ShapeDtypeStruct((m, N_GT), F32)],
        compiler_params=_cparams(("parallel", "parallel")),
        name="in_proj",
    )(x, g, w)


def _attn_kernel(lam_ref, q_ref, k_ref, vt_ref, w_ref, o_ref,
                 m1, l1, a1, m2, l2, a2, *, tk, post_scale):
    qi = pl.program_id(2)
    tq = q_ref.shape[0]

    for m, l, a in ((m1, l1, a1), (m2, l2, a2)):
        m[...] = jnp.full(m.shape, NEG, F32)
        l[...] = jnp.zeros(l.shape, F32)
        a[...] = jnp.zeros(a.shape, F32)

    q = q_ref[...]
    lane = lax.broadcasted_iota(jnp.int32, q.shape, 1)
    qcs = [jnp.where((lane >= DH_QK) == bool(c), q, jnp.zeros_like(q)) for c in range(2)]

    def step(start, diag_offset):
        start = pl.multiple_of(start, tk)
        k = k_ref[pl.ds(start, tk), :]
        vt = vt_ref[0, :, pl.ds(start, tk)]
        if diag_offset is not None:
            krow = lax.broadcasted_iota(jnp.int32, (tk, tq), 0) + diag_offset
            qcol = lax.broadcasted_iota(jnp.int32, (tk, tq), 1)
            keep = krow <= qcol
        for qc, (m, l, a) in zip(qcs, ((m1, l1, a1), (m2, l2, a2))):
            s = _dot_nt(k, qc)
            if diag_offset is not None:
                s = jnp.where(keep, s, NEG)
            m_prev = m[...]
            m_new = jnp.maximum(m_prev, jnp.max(s, axis=0, keepdims=True))
            alpha = jnp.exp2(m_prev - m_new)
            p = jnp.exp2(s - m_new)
            l[...] = alpha * l[...] + jnp.sum(p, axis=0, keepdims=True)
            a[...] = alpha * a[...] + _dot(vt, p.astype(BF16))
            m[...] = m_new

    def body(ki, carry):
        step(ki * tk, None)
        return carry

    lax.fori_loop(0, qi * (tq // tk), body, 0)
    for j in range(tq // tk):
        step(qi * tq + j * tk, j * tk)

    lam = lam_ref[0]
    ot = a1[...] / l1[...] - lam * (a2[...] / l2[...])
    o_ref[...] = _rms(jnp.transpose(ot), w_ref[...]) * post_scale


def _attn_prompt(lam, qa, ka, vat, subln_w, batch, seq, tq, tk, post_scale):
    nq = seq // tq
    return pl.pallas_call(
        functools.partial(_attn_kernel, tk=tk, post_scale=post_scale),
        grid=(batch, H_A, nq),
        in_specs=[pl.BlockSpec(memory_space=pltpu.SMEM),
                  pl.BlockSpec((tq, DV_A), lambda b, h, qi: (b * nq + qi, h)),
                  pl.BlockSpec((seq, DV_A), lambda b, h, qi: (b, h)),
                  pl.BlockSpec((1, DV_A, seq), lambda b, h, qi: (b, h, 0)),
                  pl.BlockSpec((1, DV_A), lambda b, h, qi: (0, 0))],
        out_specs=pl.BlockSpec((tq, DV_A), lambda b, h, qi: (b * nq + qi, h)),
        out_shape=jax.ShapeDtypeStruct((batch * seq, H_A * DV_A), F32),
        scratch_shapes=[pltpu.VMEM((1, tq), F32), pltpu.VMEM((1, tq), F32),
                        pltpu.VMEM((DV_A, tq), F32)] * 2,
        compiler_params=_cparams(("parallel", "parallel", "arbitrary")),
        name="attn_prompt",
    )(lam, qa, ka, vat, subln_w)


def _decode_kernel(pt_ref, lam_ref, q_ref, kn_ref, vn_ref, w_ref, *refs, n_pages, post_scale):
    k_refs = refs[:n_pages]
    v_refs = refs[n_pages:2 * n_pages]
    o_ref = refs[2 * n_pages]
    del pt_ref
    nmap = 2 * H_A
    width = N_QK
    page = k_refs[0].shape[2]

    q = q_ref[0].astype(F32)
    rows = lax.broadcasted_iota(jnp.int32, (nmap, width), 0)
    lane = lax.broadcasted_iota(jnp.int32, (nmap, width), 1)
    qrows = jnp.where(lane // DH_QK == rows, jnp.broadcast_to(q, (nmap, width)), 0.0)
    qb = qrows.astype(BF16)

    s_new = jnp.sum(qrows * kn_ref[0].astype(F32), axis=-1, keepdims=True)
    s = [_dot(qb, k_refs[i][0].astype(BF16)) for i in range(n_pages)]
    m = functools.reduce(jnp.maximum, s)
    m = jnp.maximum(jnp.max(m, axis=-1, keepdims=True), s_new)
    p = [jnp.exp2(si - m) for si in s]
    p_new = jnp.exp2(s_new - m)
    l = jnp.sum(functools.reduce(jnp.add, p), axis=-1, keepdims=True) + p_new

    r8 = lax.broadcasted_iota(jnp.int32, (nmap, page), 0)
    o8 = jnp.zeros((nmap, DV_A), F32)
    for i in range(n_pages):
        pb = p[i].astype(BF16)
        for h in range(H_A):
            vh = v_refs[i][0, pl.ds(h, page, stride=H_A), :].astype(BF16)
            o8 = o8 + _dot(jnp.where(r8 // 2 == h, pb, jnp.zeros_like(pb)), vh)

    vn = vn_ref[0].astype(BF16).astype(F32)
    r8v = lax.broadcasted_iota(jnp.int32, (nmap, DV_A), 0)
    vn8 = jnp.zeros((nmap, DV_A), F32)
    for h in range(H_A):
        vn8 = jnp.where(r8v // 2 == h, jnp.broadcast_to(vn[:, h * DV_A:(h + 1) * DV_A],
                                                        (nmap, DV_A)), vn8)
    o8 = (o8 + p_new.astype(BF16).astype(F32) * vn8) / l
    lam = lam_ref[0]
    w = w_ref[...]
    for h in range(H_A):
        d = o8[2 * h:2 * h + 1] - lam * o8[2 * h + 1:2 * h + 2]
        o_ref[0, :, h * DV_A:(h + 1) * DV_A] = _rms(d, w) * post_scale


def _attn_decode(page_table, lam, qa, ka, va, subln_w, cache_kt, cache_v, post_scale):
    nb, n_pages = page_table.shape
    width, page = cache_kt.shape[1], cache_kt.shape[2]
    one = lambda: pl.BlockSpec((1, 1, width), lambda b, pt: (b, 0, 0))
    kspec = lambda i: pl.BlockSpec((1, width, page), lambda b, pt: (pt[b, i], 0, 0))
    vspec = lambda i: pl.BlockSpec((1, page * H_A, DV_A), lambda b, pt: (pt[b, i], 0, 0))
    grid_spec = pltpu.PrefetchScalarGridSpec(
        num_scalar_prefetch=1,
        grid=(nb,),
        in_specs=[pl.BlockSpec(memory_space=pltpu.SMEM), one(), one(), one(),
                  pl.BlockSpec((1, DV_A), lambda b, pt: (0, 0))]
                 + [kspec(i) for i in range(n_pages)] + [vspec(i) for i in range(n_pages)],
        out_specs=one(),
    )
    r3 = lambda a: a.reshape(nb, 1, width)
    out = pl.pallas_call(
        functools.partial(_decode_kernel, n_pages=n_pages, post_scale=post_scale),
        grid_spec=grid_spec,
        out_shape=jax.ShapeDtypeStruct((nb, 1, width), F32),
        compiler_params=_cparams(("arbitrary",)),
        name="attn_decode",
    )(page_table, lam, r3(qa), r3(ka), r3(va), subln_w,
      *([cache_kt] * n_pages), *([cache_v] * n_pages))
    return out.reshape(nb, width)


def _hgrn_gates(qb, fb, lb):
    q = _silu(qb) * (DK_B ** -0.5)
    f = lb + (1.0 - lb) * _sigmoid(fb)
    return q, 1.0 - f, jnp.log(f)


def _hgrn_post(o, gb, w):
    return _rms(o, w) * _silu(gb)


def _hgrn_prompt_kernel(qb_ref, fb_ref, ib_ref, gb_ref, lb_ref, w_ref, o_ref, s_ref, st_ref, *,
                        chunk):
    c = chunk
    dk = DK_B
    seq = qb_ref.shape[0]
    heads = lb_ref.shape[0]
    gw = w_ref[...]

    row = lax.broadcasted_iota(jnp.int32, (c, c), 0)
    col = lax.broadcasted_iota(jnp.int32, (c, c), 1)
    tril = jnp.where(col <= row, 1.0, 0.0).astype(BF16)
    ones = jnp.ones((dk, c), BF16)
    levels = []
    m = SUBLANES
    while 2 * m <= c:
        levels.append((m, jnp.where(row // (2 * m) == col // (2 * m), 1.0, 0.0)))
        m *= 2
    diag = [jnp.where(col == (row // SUBLANES) * SUBLANES + s, 1.0, 0.0) for s in range(SUBLANES)]

    def one_head(ci, st, hh):
        rows = pl.ds(pl.multiple_of(ci * c, c), c)
        lanes = slice(hh * dk, (hh + 1) * dk)
        q, k, lf = _hgrn_gates(qb_ref[rows, lanes], fb_ref[rows, lanes], lb_ref[hh])
        v = ib_ref[rows, lanes]

        hi = lf.astype(BF16)
        r1 = lf - hi.astype(F32)
        mid = r1.astype(BF16)
        lo = (r1 - mid.astype(F32)).astype(BF16)
        g = _dot(tril, hi) + _dot(tril, mid) + _dot(tril, lo)

        o = _dot_nt((q * jnp.exp(g)).astype(BF16), st.astype(BF16))

        a = jnp.zeros((c, c), F32)
        for m, same_block in levels:
            nb = c // (2 * m)
            g3 = g.reshape(nb, 2 * m, dk)
            d = g3 - g3[:, m - 1:m, :]
            rin = lax.broadcasted_iota(jnp.int32, (nb, 2 * m, dk), 1)
            qs = jnp.where(rin >= m, q.reshape(nb, 2 * m, dk) * jnp.exp(jnp.minimum(d, 0.0)), 0.0)
            ks = jnp.where(rin < m, k.reshape(nb, 2 * m, dk) * jnp.exp(jnp.minimum(-d, 0.0)), 0.0)
            al = _dot_nt(qs.reshape(c, dk).astype(BF16), ks.reshape(c, dk).astype(BF16))
            a = a + al * same_block

        nb = c // SUBLANES
        g8 = g.reshape(nb, SUBLANES, dk)
        q8 = q.reshape(nb, SUBLANES, dk)
        k8 = k.reshape(nb, SUBLANES, dk)
        rin = lax.broadcasted_iota(jnp.int32, (nb, SUBLANES, dk), 1)
        for s in range(SUBLANES):
            d = g8 - g8[:, s:s + 1, :]
            p = jnp.where(rin >= s, q8 * k8[:, s:s + 1, :] * jnp.exp(jnp.minimum(d, 0.0)), 0.0)
            r = _dot(p.reshape(c, dk).astype(BF16), ones)
            a = a + r * diag[s]

        o = o + _dot(a.astype(BF16), v.astype(BF16))
        o_ref[rows, lanes] = _hgrn_post(o, gb_ref[rows, lanes], gw)

        g_end = g[c - 1:c, :]
        kd = (k * jnp.exp(g_end - g)).astype(BF16)
        return st * jnp.exp(g_end) + _dot(jnp.transpose(v).astype(BF16), kd)

    def body(ci, sts):
        return tuple(one_head(ci, st, hh) for hh, st in enumerate(sts))

    sb = pl.program_id(1)

    @pl.when(sb == 0)
    def _():
        st_ref[...] = jnp.zeros(st_ref.shape, F32)

    sts = lax.fori_loop(0, seq // c, body, tuple(st_ref[hh] for hh in range(heads)))
    for hh, st in enumerate(sts):
        st_ref[hh] = st

    @pl.when(sb == pl.num_programs(1) - 1)
    def _():
        for hh, st in enumerate(sts):
            s_ref[0, hh] = jnp.transpose(st)


def _hgrn_prompt(hg, lb, gnorm_w, batch, seq, chunk, rows):
    nsb = seq // rows
    width = H_B * DK_B
    blk = lambda j: pl.BlockSpec((rows, width), lambda b, s: (b * nsb + s, j))
    return pl.pallas_call(
        functools.partial(_hgrn_prompt_kernel, chunk=chunk),
        grid=(batch, nsb),
        in_specs=[blk(0), blk(1), blk(2), blk(3),
                  pl.BlockSpec((H_B, 1, DK_B), lambda b, s: (0, 0, 0)),
                  pl.BlockSpec((1, DV_B), lambda b, s: (0, 0))],
        out_specs=[pl.BlockSpec((rows, H_B * DV_B), lambda b, s: (b * nsb + s, 0)),
                   pl.BlockSpec((1, H_B, DK_B, DV_B), lambda b, s: (b, 0, 0, 0))],
        out_shape=[jax.ShapeDtypeStruct((batch * seq, H_B * DV_B), F32),
                   jax.ShapeDtypeStruct((batch, H_B, DK_B, DV_B), F32)],
        scratch_shapes=[pltpu.VMEM((H_B, DV_B, DK_B), F32)],
        compiler_params=_cparams(("parallel", "arbitrary")),
        name="hgrn_prompt",
    )(hg, hg, hg, hg, lb, gnorm_w)


def _hgrn_sample_kernel(hg_ref, lb_ref, w_ref, s0_ref, o_ref, s_ref):
    nb = s0_ref.shape[0]
    for b in range(nb):
        for h in range(H_B):
            sl = lambda j: slice((j * H_B + h) * DK_B, (j * H_B + h + 1) * DK_B)
            row = lambda j: hg_ref[b, :, sl(j)]
            q, k, lf = _hgrn_gates(row(0), row(1), lb_ref[h])
            v = row(2)
            stack = jnp.concatenate(
                [q, k, jnp.exp(lf), jnp.zeros((DK_B - 3, DK_B), F32)], axis=0)
            cols = jnp.transpose(stack)
            s_new = cols[:, 2:3] * s0_ref[b, h] + cols[:, 1:2] * v
            s_ref[b, h] = s_new
            o = jnp.sum(cols[:, 0:1] * s_new, axis=0, keepdims=True)
            o_ref[b, :, h * DV_B:(h + 1) * DV_B] = _hgrn_post(o, row(3), w_ref[...])


def _hgrn_sample(hg, lb, gnorm_w, state, tb):
    nb = state.shape[0]
    return pl.pallas_call(
        _hgrn_sample_kernel,
        grid=(nb // tb,),
        in_specs=[pl.BlockSpec((tb, 1, N_HG), lambda i: (i, 0, 0)),
                  pl.BlockSpec((H_B, 1, DK_B), lambda i: (0, 0, 0)),
                  pl.BlockSpec((1, DV_B), lambda i: (0, 0)),
                  pl.BlockSpec((tb, H_B, DK_B, DV_B), lambda i: (i, 0, 0, 0))],
        out_specs=[pl.BlockSpec((tb, 1, H_B * DV_B), lambda i: (i, 0, 0)),
                   pl.BlockSpec((tb, H_B, DK_B, DV_B), lambda i: (i, 0, 0, 0))],
        out_shape=[jax.ShapeDtypeStruct((nb, 1, H_B * DV_B), F32),
                   jax.ShapeDtypeStruct(state.shape, F32)],
        compiler_params=_cparams(("parallel",)),
        name="hgrn_sample",
    )(hg.reshape(nb, 1, N_HG), lb, gnorm_w, state)


def _merge_kernel(x_ref, oa_ref, ob_ref, gt_ref, wa_ref, wb_ref, wo_ref, n2w_ref, wq_ref,
                  keys_ref, h_ref, nt_ref, s1_ref, s2_ref):
    ga = gt_ref[:, :D_MODEL]
    gb = gt_ref[:, D_MODEL:]
    m = (_sigmoid(ga) * _dot(oa_ref[...].astype(BF16), wa_ref[...])
         + _sigmoid(gb) * _dot(ob_ref[...].astype(BF16), wb_ref[...]))
    h = x_ref[...] + _dot(m.astype(BF16), wo_ref[...])
    h_ref[...] = h
    n2 = _rms(h, n2w_ref[...])
    nt_ref[...] = jnp.transpose(n2).astype(BF16)
    qp = _dot(n2.astype(BF16), wq_ref[...]).astype(BF16)
    for hh in range(H_P):
        qh = qp[:, hh * LANES:(hh + 1) * LANES]
        rows = pl.ds(hh, N_KEYS, stride=H_P)
        for c, s_ref in enumerate((s1_ref, s2_ref)):
            s = _dot_nt(keys_ref[c, hh], qh)
            for blk in range(s_ref.shape[0]):
                s_ref[blk, rows, :] = s[:, blk * LANES:(blk + 1) * LANES]


def _merge(x, oa, ob, gt, wa, wb, wo, n2w, wq, keys, tm):
    t = x.shape[0]
    row = lambda n: pl.BlockSpec((tm, n), lambda i: (i, 0))
    full = lambda a: pl.BlockSpec(a.shape, lambda i: (0,) * a.ndim)
    sspec = pl.BlockSpec((tm // LANES, N_KEYS * H_P, LANES), lambda i: (i, 0, 0))
    return pl.pallas_call(
        _merge_kernel,
        grid=(t // tm,),
        in_specs=[row(D_MODEL), row(N_QK), row(H_B * DV_B), row(N_GT),
                  full(wa), full(wb), full(wo), full(n2w), full(wq), full(keys)],
        out_specs=[row(D_MODEL), pl.BlockSpec((D_MODEL, tm), lambda i: (0, i)), sspec, sspec],
        out_shape=[jax.ShapeDtypeStruct((t, D_MODEL), F32),
                   jax.ShapeDtypeStruct((D_MODEL, t), BF16),
                   jax.ShapeDtypeStruct((t // LANES, N_KEYS * H_P, LANES), F32),
                   jax.ShapeDtypeStruct((t // LANES, N_KEYS * H_P, LANES), F32)],
        compiler_params=_cparams(("parallel",)),
        name="merge",
    )(x, oa, ob, gt, wa, wb, wo, n2w, wq, keys)


def _bitonic_merge_desc(a):
    a = list(a)
    d = len(a) // 2
    while d >= 1:
        for i in range(len(a)):
            if i & d == 0:
                a[i], a[i + d] = jnp.maximum(a[i], a[i + d]), jnp.minimum(a[i], a[i + d])
        d //= 2
    return a


def _sort_desc(a):
    if len(a) == 1:
        return list(a)
    half = len(a) // 2
    return _bitonic_merge_desc(_sort_desc(a[:half]) + _sort_desc(a[half:])[::-1])


def _top16(vals):
    groups = [_sort_desc(vals[i:i + TOPK_P]) for i in range(0, len(vals), TOPK_P)]
    while len(groups) > 1:
        groups = [_bitonic_merge_desc([jnp.maximum(a[i], b[TOPK_P - 1 - i])
                                       for i in range(TOPK_P)])
                  for a, b in zip(groups[0::2], groups[1::2])]
    return groups[0]


def _topk_kernel(s1_ref, s2_ref, cnt_ref, e1_ref, r2_ref, e2_ref):
    tl = s1_ref.shape[2]
    key_rows = lambda ref, e: ref[0, pl.ds(e * H_P, H_P), :]
    t1 = _top16([key_rows(s1_ref, e) for e in range(N_KEYS)])
    t2 = _top16([key_rows(s2_ref, e) for e in range(N_KEYS)])
    pairs = [(p, r) for p in range(TOPK_P) for r in range(TOPK_P) if (p + 1) * (r + 1) <= TOPK_P]
    sums = {pr: t1[pr[0]] + t2[pr[1]] for pr in pairs}
    pad = [jnp.full((H_P, tl), NEG, F32)] * (-len(pairs) % (2 * TOPK_P))
    top = _top16([sums[pr] for pr in pairs] + pad)
    tau = top[TOPK_P - 1]
    z = functools.reduce(jnp.add, [jnp.exp(c - top[0]) for c in top])
    theta = []
    for r in range(TOPK_P):
        th = jnp.full((H_P, tl), -NEG, F32)
        for p in range(TOPK_P // (r + 1)):
            th = jnp.where(sums[(p, r)] >= tau, t1[p], th)
        theta.append(th)
    shift1 = t1[0] + jnp.log(z)
    for e in range(N_KEYS):
        x = key_rows(s1_ref, e)
        cnt = jnp.zeros((H_P, tl), F32)
        for r in range(TOPK_P):
            cnt = jnp.where(x >= theta[r], r + 1.0, cnt)
        cnt_ref[pl.ds(e * H_P, H_P), :] = cnt
        e1_ref[pl.ds(e * H_P, H_P), :] = jnp.exp(x - shift1)
    for h in range(H_P):
        xh = s2_ref[0, pl.ds(h, N_KEYS, stride=H_P), :]
        rank = jnp.zeros((N_KEYS, tl), F32)
        for r in range(TOPK_P):
            rank = jnp.where(xh < t2[r][h:h + 1, :], r + 1.0, rank)
        r2_ref[h] = rank.astype(BF16)
        e2_ref[h] = jnp.exp(xh - t2[0][h:h + 1, :]).astype(BF16)


def _topk(s1, s2):
    tl = s1.shape[2]
    t = s1.shape[0] * tl
    sspec = pl.BlockSpec((N_KEYS * H_P, tl), lambda i: (0, i))
    hspec = pl.BlockSpec((H_P, N_KEYS, tl), lambda i: (0, 0, i))
    ispec = pl.BlockSpec((1, N_KEYS * H_P, tl), lambda i: (i, 0, 0))
    return pl.pallas_call(
        _topk_kernel,
        grid=(t // tl,),
        in_specs=[ispec, ispec],
        out_specs=[sspec, sspec, hspec, hspec],
        out_shape=[jax.ShapeDtypeStruct((N_KEYS * H_P, t), F32),
                   jax.ShapeDtypeStruct((N_KEYS * H_P, t), F32),
                   jax.ShapeDtypeStruct((H_P, N_KEYS, t), BF16),
                   jax.ShapeDtypeStruct((H_P, N_KEYS, t), BF16)],
        compiler_params=_cparams(("parallel",)),
        name="topk",
    )(s1, s2)


def _gelu(x):
    return 0.5 * x * (1.0 + lax.erf(x * (2.0 ** -0.5)))


def _peer_kernel(xt_ref, u_ref, vt_ref, cnt_ref, e1_ref, r2_ref, e2_ref, h_ref, nfw_ref,
                 y_ref, acc_ref, *, e1_per_step):
    j = pl.program_id(1)
    nj = pl.num_programs(1)
    tm = xt_ref.shape[1]
    pk = 2 * SUBLANES

    def row_tile(ref, row):
        return jnp.broadcast_to(ref[row, :], (pk, tm)).astype(BF16)[None]

    def weights(e1):
        w = None
        for hh in range(H_P):
            row = pl.ds(e1 * H_P + hh, 1)
            cnt = row_tile(cnt_ref, row)
            e2 = e2_ref[hh].reshape(N_KEYS // pk, pk, tm)
            r2 = r2_ref[hh].reshape(N_KEYS // pk, pk, tm)
            term = jnp.where(r2 < cnt, e2, jnp.zeros_like(e2)) * row_tile(e1_ref, row)
            w = term if w is None else w + term
        return w.reshape(N_KEYS, tm)

    @pl.when(j == 0)
    def _():
        acc_ref[...] = jnp.zeros(acc_ref.shape, F32)

    w = jnp.concatenate([weights(j * e1_per_step + a) for a in range(e1_per_step)], axis=0)
    act = _gelu(_dot(u_ref[...], xt_ref[...]))
    acc_ref[...] += _dot(vt_ref[...], w * act.astype(BF16))

    @pl.when(j == nj - 1)
    def _():
        y_ref[...] = _rms(h_ref[...] + jnp.transpose(acc_ref[...]), nfw_ref[...])


def _peer(xt, u, v, cnt, e1w, r2, e2w, h, nfw, tm, e1_per_step):
    t = h.shape[0]
    te = e1_per_step * N_KEYS
    kspec = pl.BlockSpec((N_KEYS * H_P, tm), lambda i, j: (0, i))
    hspec = pl.BlockSpec((H_P, N_KEYS, tm), lambda i, j: (0, 0, i))
    row = pl.BlockSpec((tm, D_MODEL), lambda i, j: (i, 0))
    return pl.pallas_call(
        functools.partial(_peer_kernel, e1_per_step=e1_per_step),
        grid=(t // tm, N_KEYS // e1_per_step),
        in_specs=[pl.BlockSpec((D_MODEL, tm), lambda i, j: (0, i)),
                  pl.BlockSpec((te, D_MODEL), lambda i, j: (j, 0)),
                  pl.BlockSpec((D_MODEL, te), lambda i, j: (0, j)),
                  kspec, kspec, hspec, hspec, row,
                  pl.BlockSpec((1, D_MODEL), lambda i, j: (0, 0))],
        out_specs=row,
        out_shape=jax.ShapeDtypeStruct((t, D_MODEL), F32),
        scratch_shapes=[pltpu.VMEM((D_MODEL, tm), F32)],
        compiler_params=_cparams(("parallel", "arbitrary")),
        name="peer",
    )(xt, u, v, cnt, e1w, r2, e2w, h, nfw)


def _tile(n, pref):
    t = min(n, pref)
    assert n % t == 0, (n, t)
    return t


def kernel(x_prompt, x_sample, cache_k, cache_v, state_hgrn, page_table, norm1_w, w_in,
           lambda_q1, lambda_k1, lambda_q2, lambda_k2, subln_w, lb_param, gnorm_w,
           w_branch_a, w_branch_b, w_out, norm2_w, w_query, sub_keys, expert_u, expert_v,
           norm_f_w):
    depth = w_in.shape[0]
    assert depth == 1 and w_in.shape[2] == N_IN
    batch, seq, _ = x_prompt.shape
    n_dec, dec_len, _ = x_sample.shape
    assert dec_len == 1
    l = 0

    lam_init = 0.8 - 0.6 * math.exp(-0.3 * l)
    post_scale = 1.0 - lam_init
    f = F32
    lam = (jnp.exp(jnp.sum(lambda_q1[l].astype(f) * lambda_k1[l].astype(f)))
           - jnp.exp(jnp.sum(lambda_q2[l].astype(f) * lambda_k2[l].astype(f)))
           + lam_init).reshape(1)
    lb = jnp.cumsum(jax.nn.softmax(lb_param.astype(f), axis=0), axis=0)[l].reshape(H_B, 1, DK_B)

    n1w = norm1_w[l].reshape(1, D_MODEL)
    n2w = norm2_w[l].reshape(1, D_MODEL)
    nfw = norm_f_w.reshape(1, D_MODEL)
    sub_w = subln_w[l].reshape(1, DV_A)
    gn_w = gnorm_w[l].reshape(1, DV_B)
    win = w_in[l].astype(BF16)
    wa = w_branch_a[l].astype(BF16)
    wb = w_branch_b[l].astype(BF16)
    wo = w_out[l].astype(BF16)
    wq = w_query[l].astype(BF16)
    sk = jnp.transpose(sub_keys[l].astype(BF16), (1, 0, 2, 3))
    zk = jnp.zeros_like(sk[0])
    keys = jnp.stack([jnp.concatenate([sk[0], zk], axis=-1),
                      jnp.concatenate([zk, sk[1]], axis=-1)])
    u = expert_u.reshape(expert_u.shape[1:]).astype(BF16)
    v = jnp.transpose(expert_v.reshape(expert_v.shape[1:])).astype(BF16)

    def tail(x2, oa, ob, gt, tm_merge, tm_peer, e1_per_step=16):
        h, nt, s1, s2 = _merge(x2, oa, ob, gt, wa, wb, wo, n2w, wq, keys, tm_merge)
        cnt, e1w, r2, e2w = _topk(s1, s2)
        return _peer(nt, u, v, cnt, e1w, r2, e2w, h, nfw, tm_peer, e1_per_step)


    def keys_out(kt, b, s):
        return jnp.transpose(kt.reshape(1, b, H_A, 2, DH_QK, s), (0, 1, 5, 2, 3, 4))

    t = batch * seq
    xp = x_prompt.reshape(t, D_MODEL)
    qa, ka, kt, va, vat, hg, gt = _in_proj(xp, n1w, win, batch, _tile(seq, 256))
    tq = _tile(seq, 1024)
    oa = _attn_prompt(lam, qa, ka, vat, sub_w, batch, seq, tq, _tile(tq, 1024), post_scale)
    ob, s_p = _hgrn_prompt(hg, lb, gn_w, batch, seq, _tile(seq, 128), _tile(seq, 1024))
    y_p = tail(xp, oa, ob, gt, _tile(t, 256), _tile(t, 512))

    xs = x_sample.reshape(n_dec, D_MODEL)
    qa_s, ka_s, kt_s, va_s, _, hg_s, gt_s = _in_proj(xs, n1w, win, 1, _tile(n_dec, 128))
    n_phys, page = cache_k.shape[1], cache_k.shape[2]
    cache_kt = jnp.transpose(cache_k, (0, 1, 3, 4, 5, 2)).reshape(n_phys, N_QK, page)
    cache_v2 = cache_v.reshape(n_phys, page * H_A, DV_A)
    oa_s = _attn_decode(page_table, lam, qa_s, ka_s, va_s, sub_w, cache_kt, cache_v2, post_scale)
    ob_s, s_s = _hgrn_sample(hg_s, lb, gn_w, state_hgrn.reshape(state_hgrn.shape[1:]),
                             _tile(n_dec, 8))
    y_s = tail(xs, oa_s, ob_s.reshape(n_dec, H_B * DV_B), gt_s,
               _tile(n_dec, 128), _tile(n_dec, 128))

    return (y_p.reshape(batch, seq, D_MODEL),
            y_s.reshape(n_dec, 1, D_MODEL),
            keys_out(kt, batch, seq),
            va.reshape(1, batch, seq, H_A, DV_A),
            s_p.reshape(1, batch, H_B, DK_B, DV_B),
            jnp.transpose(keys_out(kt_s, 1, n_dec), (0, 2, 1, 3, 4, 5)),
            va_s.reshape(1, n_dec, 1, H_A, DV_A),
            s_s.reshape(1, n_dec, H_B, DK_B, DV_B))
```

```python
import functools
import math

import jax
import jax.numpy as jnp
from jax import lax
from jax.experimental import pallas as pl
from jax.experimental.pallas import tpu as pltpu

F32 = jnp.float32
BF16 = jnp.bfloat16

EPS = 1e-6
NEG = -1e30
LOG2E = math.log2(math.e)
LANES = 128
SUBLANES = 8
VMEM_LIMIT = 56 * 1024 * 1024

H_A = 4
DH_QK = 64
DV_A = 128
H_B = 4
DK_B = 128
DV_B = 128
N_KEYS = 128
H_P = 8
DK_P_HALF = 64
TOPK_P = 16
D_MODEL = 1024
N_QK = H_A * 2 * DH_QK
N_HG = 4 * H_B * DK_B
N_GT = 2 * D_MODEL
N_IN = 3 * N_QK + N_HG + N_GT


def _cparams(sem):
    return pltpu.CompilerParams(dimension_semantics=sem, vmem_limit_bytes=VMEM_LIMIT)


def _rms(x, w):
    return x * lax.rsqrt(jnp.mean(x * x, axis=-1, keepdims=True) + EPS) * w


def _sigmoid(x):
    return 1.0 / (1.0 + jnp.exp(-x))


def _silu(x):
    return x * _sigmoid(x)


def _dot(a, b):
    return jnp.dot(a, b, preferred_element_type=F32)


def _dot_nt(a, b):
    return lax.dot_general(a, b, (((1,), (1,)), ((), ())), preferred_element_type=F32)


def _inproj_kernel(x_ref, g_ref, w_ref, qa_ref, ka_ref, kt_ref, va_ref, vt_ref,
                   hg_ref, gt_ref):
    nb = _rms(x_ref[...], g_ref[...]).astype(BF16)
    cw = N_QK

    def mm(c):
        return _dot(nb, w_ref[:, c * cw:(c + 1) * cw])

    qa_ref[...] = (mm(0) * (DH_QK ** -0.5 * LOG2E)).astype(BF16)
    ka = mm(1)
    ka_ref[...] = ka.astype(BF16)
    kt_ref[0] = jnp.transpose(ka)
    va = mm(2)
    for h in range(H_A):
        va_ref[pl.ds(h, va.shape[0], stride=H_A), :] = va[:, h * DV_A:(h + 1) * DV_A]
    vt_ref[0] = jnp.transpose(va).astype(BF16)
    for c in range(N_HG // cw):
        hg_ref[:, c * cw:(c + 1) * cw] = mm(3 + c)
    for c in range(N_GT // cw):
        gt_ref[:, c * cw:(c + 1) * cw] = mm(3 + N_HG // cw + c)


def _in_proj(x, g, w, batch, tm):
    m = x.shape[0]
    seq = m // batch
    nt = seq // tm
    row = lambda n: pl.BlockSpec((tm, n), lambda b, i: (b * nt + i, 0))
    full = lambda a: pl.BlockSpec(a.shape, lambda b, i: (0, 0))
    tspec = pl.BlockSpec((1, N_QK, tm), lambda b, i: (b, 0, i))
    return pl.pallas_call(
        _inproj_kernel,
        grid=(batch, nt),
        in_specs=[row(D_MODEL), full(g), full(w)],
        out_specs=[row(N_QK), row(N_QK), tspec,
                   pl.BlockSpec((tm * H_A, DV_A), lambda b, i: (b * nt + i, 0)),
                   tspec, row(N_HG), row(N_GT)],
        out_shape=[jax.ShapeDtypeStruct((m, N_QK), BF16),
                   jax.ShapeDtypeStruct((m, N_QK), BF16),
                   jax.ShapeDtypeStruct((batch, N_QK, seq), F32),
                   jax.ShapeDtypeStruct((m * H_A, DV_A), F32),
                   jax.ShapeDtypeStruct((batch, N_QK, seq), BF16),
                   jax.ShapeDtypeStruct((m, N_HG), F32),
                   jax.ShapeDtypeStruct((m, N_GT), F32)],
        compiler_params=_cparams(("parallel", "parallel")),
        name="in_proj",
    )(x, g, w)


def _attn_kernel(lam_ref, q_ref, k_ref, vt_ref, w_ref, o_ref,
                 m1, l1, a1, m2, l2, a2, *, tk, post_scale):
    qi = pl.program_id(2)
    tq = q_ref.shape[0]

    for m, l, a in ((m1, l1, a1), (m2, l2, a2)):
        m[...] = jnp.full(m.shape, NEG, F32)
        l[...] = jnp.zeros(l.shape, F32)
        a[...] = jnp.zeros(a.shape, F32)

    q = q_ref[...]
    lane = lax.broadcasted_iota(jnp.int32, q.shape, 1)
    qcs = [jnp.where((lane >= DH_QK) == bool(c), q, jnp.zeros_like(q)) for c in range(2)]

    def step(start, diag_offset):
        start = pl.multiple_of(start, tk)
        k = k_ref[pl.ds(start, tk), :]
        vt = vt_ref[0, :, pl.ds(start, tk)]
        if diag_offset is not None:
            krow = lax.broadcasted_iota(jnp.int32, (tk, tq), 0) + diag_offset
            qcol = lax.broadcasted_iota(jnp.int32, (tk, tq), 1)
            keep = krow <= qcol
        for qc, (m, l, a) in zip(qcs, ((m1, l1, a1), (m2, l2, a2))):
            s = _dot_nt(k, qc)
            if diag_offset is not None:
                s = jnp.where(keep, s, NEG)
            m_prev = m[...]
            m_new = jnp.maximum(m_prev, jnp.max(s, axis=0, keepdims=True))
            alpha = jnp.exp2(m_prev - m_new)
            p = jnp.exp2(s - m_new)
            l[...] = alpha * l[...] + jnp.sum(p, axis=0, keepdims=True)
            a[...] = alpha * a[...] + _dot(vt, p.astype(BF16))
            m[...] = m_new

    def body(ki, carry):
        step(ki * tk, None)
        return carry

    lax.fori_loop(0, qi * (tq // tk), body, 0)
    for j in range(tq // tk):
        step(qi * tq + j * tk, j * tk)

    lam = lam_ref[0]
    ot = a1[...] / l1[...] - lam * (a2[...] / l2[...])
    o_ref[...] = _rms(jnp.transpose(ot), w_ref[...]) * post_scale


def _attn_prompt(lam, qa, ka, vat, subln_w, batch, seq, tq, tk, post_scale):
    nq = seq // tq
    return pl.pallas_call(
        functools.partial(_attn_kernel, tk=tk, post_scale=post_scale),
        grid=(batch, H_A, nq),
        in_specs=[pl.BlockSpec(memory_space=pltpu.SMEM),
                  pl.BlockSpec((tq, DV_A), lambda b, h, qi: (b * nq + qi, h)),
                  pl.BlockSpec((seq, DV_A), lambda b, h, qi: (b, h)),
                  pl.BlockSpec((1, DV_A, seq), lambda b, h, qi: (b, h, 0)),
                  pl.BlockSpec((1, DV_A), lambda b, h, qi: (0, 0))],
        out_specs=pl.BlockSpec((tq, DV_A), lambda b, h, qi: (b * nq + qi, h)),
        out_shape=jax.ShapeDtypeStruct((batch * seq, H_A * DV_A), F32),
        scratch_shapes=[pltpu.VMEM((1, tq), F32), pltpu.VMEM((1, tq), F32),
                        pltpu.VMEM((DV_A, tq), F32)] * 2,
        compiler_params=_cparams(("parallel", "parallel", "arbitrary")),
        name="attn_prompt",
    )(lam, qa, ka, vat, subln_w)


def _decode_kernel(pt_ref, lam_ref, q_ref, kn_ref, vn_ref, w_ref, *refs, n_pages, post_scale):
    k_refs = refs[:n_pages]
    v_refs = refs[n_pages:2 * n_pages]
    o_ref = refs[2 * n_pages]
    del pt_ref
    nmap = 2 * H_A
    width = N_QK
    page = k_refs[0].shape[2]

    q = q_ref[0].astype(F32)
    rows = lax.broadcasted_iota(jnp.int32, (nmap, width), 0)
    lane = lax.broadcasted_iota(jnp.int32, (nmap, width), 1)
    qrows = jnp.where(lane // DH_QK == rows, jnp.broadcast_to(q, (nmap, width)), 0.0)
    qb = qrows.astype(BF16)

    s_new = jnp.sum(qrows * kn_ref[0].astype(F32), axis=-1, keepdims=True)
    s = [_dot(qb, k_refs[i][0].astype(BF16)) for i in range(n_pages)]
    m = functools.reduce(jnp.maximum, s)
    m = jnp.maximum(jnp.max(m, axis=-1, keepdims=True), s_new)
    p = [jnp.exp2(si - m) for si in s]
    p_new = jnp.exp2(s_new - m)
    l = jnp.sum(functools.reduce(jnp.add, p), axis=-1, keepdims=True) + p_new

    r8 = lax.broadcasted_iota(jnp.int32, (nmap, page), 0)
    o8 = jnp.zeros((nmap, DV_A), F32)
    for i in range(n_pages):
        pb = p[i].astype(BF16)
        for h in range(H_A):
            vh = v_refs[i][0, pl.ds(h, page, stride=H_A), :].astype(BF16)
            o8 = o8 + _dot(jnp.where(r8 // 2 == h, pb, jnp.zeros_like(pb)), vh)

    vn = vn_ref[0].astype(BF16).astype(F32)
    r8v = lax.broadcasted_iota(jnp.int32, (nmap, DV_A), 0)
    vn8 = jnp.zeros((nmap, DV_A), F32)
    for h in range(H_A):
        vn8 = jnp.where(r8v // 2 == h, jnp.broadcast_to(vn[h:h + 1, :], (nmap, DV_A)), vn8)
    o8 = (o8 + p_new.astype(BF16).astype(F32) * vn8) / l
    lam = lam_ref[0]
    w = w_ref[...]
    for h in range(H_A):
        d = o8[2 * h:2 * h + 1] - lam * o8[2 * h + 1:2 * h + 2]
        o_ref[0, :, h * DV_A:(h + 1) * DV_A] = _rms(d, w) * post_scale


def _attn_decode(page_table, lam, qa, ka, va, subln_w, cache_kt, cache_v, post_scale):
    nb, n_pages = page_table.shape
    width, page = cache_kt.shape[1], cache_kt.shape[2]
    one = lambda: pl.BlockSpec((1, 1, width), lambda b, pt: (b, 0, 0))
    kspec = lambda i: pl.BlockSpec((1, width, page), lambda b, pt: (pt[b, i], 0, 0))
    vspec = lambda i: pl.BlockSpec((1, page * H_A, DV_A), lambda b, pt: (pt[b, i], 0, 0))
    grid_spec = pltpu.PrefetchScalarGridSpec(
        num_scalar_prefetch=1,
        grid=(nb,),
        in_specs=[pl.BlockSpec(memory_space=pltpu.SMEM), one(), one(),
                  pl.BlockSpec((1, H_A, DV_A), lambda b, pt: (b, 0, 0)),
                  pl.BlockSpec((1, DV_A), lambda b, pt: (0, 0))]
                 + [kspec(i) for i in range(n_pages)] + [vspec(i) for i in range(n_pages)],
        out_specs=one(),
    )
    r3 = lambda a: a.reshape(nb, 1, width)
    out = pl.pallas_call(
        functools.partial(_decode_kernel, n_pages=n_pages, post_scale=post_scale),
        grid_spec=grid_spec,
        out_shape=jax.ShapeDtypeStruct((nb, 1, width), F32),
        compiler_params=_cparams(("arbitrary",)),
        name="attn_decode",
    )(page_table, lam, r3(qa), r3(ka), va.reshape(nb, H_A, DV_A), subln_w,
      *([cache_kt] * n_pages), *([cache_v] * n_pages))
    return out.reshape(nb, width)


def _hgrn_gates(qb, fb, lb):
    q = _silu(qb) * (DK_B ** -0.5)
    f = lb + (1.0 - lb) * _sigmoid(fb)
    return q, 1.0 - f, jnp.log(f)


def _hgrn_post(o, gb, w):
    return _rms(o, w) * _silu(gb)


def _hgrn_prompt_kernel(qb_ref, fb_ref, ib_ref, gb_ref, lb_ref, w_ref, o_ref, s_ref, st_ref, *,
                        chunk):
    c = chunk
    dk = DK_B
    seq = qb_ref.shape[0]
    heads = lb_ref.shape[0]
    gw = w_ref[...]

    row = lax.broadcasted_iota(jnp.int32, (c, c), 0)
    col = lax.broadcasted_iota(jnp.int32, (c, c), 1)
    tril = jnp.where(col <= row, 1.0, 0.0).astype(BF16)
    ones = jnp.ones((dk, c), BF16)
    levels = []
    m = SUBLANES
    while 2 * m <= c:
        levels.append((m, jnp.where(row // (2 * m) == col // (2 * m), 1.0, 0.0)))
        m *= 2
    diag = [jnp.where(col == (row // SUBLANES) * SUBLANES + s, 1.0, 0.0) for s in range(SUBLANES)]

    def one_head(ci, st, hh):
        rows = pl.ds(pl.multiple_of(ci * c, c), c)
        lanes = slice(hh * dk, (hh + 1) * dk)
        q, k, lf = _hgrn_gates(qb_ref[rows, lanes], fb_ref[rows, lanes], lb_ref[hh])
        v = ib_ref[rows, lanes]

        hi = lf.astype(BF16)
        r1 = lf - hi.astype(F32)
        mid = r1.astype(BF16)
        lo = (r1 - mid.astype(F32)).astype(BF16)
        g = _dot(tril, hi) + _dot(tril, mid) + _dot(tril, lo)

        o = _dot_nt((q * jnp.exp(g)).astype(BF16), st.astype(BF16))

        a = jnp.zeros((c, c), F32)
        for m, same_block in levels:
            nb = c // (2 * m)
            g3 = g.reshape(nb, 2 * m, dk)
            d = g3 - g3[:, m - 1:m, :]
            rin = lax.broadcasted_iota(jnp.int32, (nb, 2 * m, dk), 1)
            qs = jnp.where(rin >= m, q.reshape(nb, 2 * m, dk) * jnp.exp(jnp.minimum(d, 0.0)), 0.0)
            ks = jnp.where(rin < m, k.reshape(nb, 2 * m, dk) * jnp.exp(jnp.minimum(-d, 0.0)), 0.0)
            al = _dot_nt(qs.reshape(c, dk).astype(BF16), ks.reshape(c, dk).astype(BF16))
            a = a + al * same_block

        nb = c // SUBLANES
        g8 = g.reshape(nb, SUBLANES, dk)
        q8 = q.reshape(nb, SUBLANES, dk)
        k8 = k.reshape(nb, SUBLANES, dk)
        rin = lax.broadcasted_iota(jnp.int32, (nb, SUBLANES, dk), 1)
        for s in range(SUBLANES):
            d = g8 - g8[:, s:s + 1, :]
            p = jnp.where(rin >= s, q8 * k8[:, s:s + 1, :] * jnp.exp(jnp.minimum(d, 0.0)), 0.0)
            r = _dot(p.reshape(c, dk).astype(BF16), ones)
            a = a + r * diag[s]

        o = o + _dot(a.astype(BF16), v.astype(BF16))
        o_ref[rows, lanes] = _hgrn_post(o, gb_ref[rows, lanes], gw)

        g_end = g[c - 1:c, :]
        kd = (k * jnp.exp(g_end - g)).astype(BF16)
        return st * jnp.exp(g_end) + _dot(jnp.transpose(v).astype(BF16), kd)

    def body(ci, sts):
        return tuple(one_head(ci, st, hh) for hh, st in enumerate(sts))

    sb = pl.program_id(1)

    @pl.when(sb == 0)
    def _():
        st_ref[...] = jnp.zeros(st_ref.shape, F32)

    sts = lax.fori_loop(0, seq // c, body, tuple(st_ref[hh] for hh in range(heads)))
    for hh, st in enumerate(sts):
        st_ref[hh] = st

    @pl.when(sb == pl.num_programs(1) - 1)
    def _():
        for hh, st in enumerate(sts):
            s_ref[0, hh] = jnp.transpose(st)


def _hgrn_prompt(hg, lb, gnorm_w, batch, seq, chunk, rows):
    nsb = seq // rows
    width = H_B * DK_B
    blk = lambda j: pl.BlockSpec((rows, width), lambda b, s: (b * nsb + s, j))
    return pl.pallas_call(
        functools.partial(_hgrn_prompt_kernel, chunk=chunk),
        grid=(batch, nsb),
        in_specs=[blk(0), blk(1), blk(2), blk(3),
                  pl.BlockSpec((H_B, 1, DK_B), lambda b, s: (0, 0, 0)),
                  pl.BlockSpec((1, DV_B), lambda b, s: (0, 0))],
        out_specs=[pl.BlockSpec((rows, H_B * DV_B), lambda b, s: (b * nsb + s, 0)),
                   pl.BlockSpec((1, H_B, DK_B, DV_B), lambda b, s: (b, 0, 0, 0))],
        out_shape=[jax.ShapeDtypeStruct((batch * seq, H_B * DV_B), F32),
                   jax.ShapeDtypeStruct((batch, H_B, DK_B, DV_B), F32)],
        scratch_shapes=[pltpu.VMEM((H_B, DV_B, DK_B), F32)],
        compiler_params=_cparams(("parallel", "arbitrary")),
        name="hgrn_prompt",
    )(hg, hg, hg, hg, lb, gnorm_w)


def _hgrn_sample_kernel(hg_ref, lb_ref, w_ref, s0_ref, o_ref, s_ref):
    nb = s0_ref.shape[0]
    for b in range(nb):
        for h in range(H_B):
            sl = lambda j: slice((j * H_B + h) * DK_B, (j * H_B + h + 1) * DK_B)
            row = lambda j: hg_ref[b, :, sl(j)]
            q, k, lf = _hgrn_gates(row(0), row(1), lb_ref[h])
            v = row(2)
            stack = jnp.concatenate(
                [q, k, jnp.exp(lf), jnp.zeros((DK_B - 3, DK_B), F32)], axis=0)
            cols = jnp.transpose(stack)
            s_new = cols[:, 2:3] * s0_ref[b, h] + cols[:, 1:2] * v
            s_ref[b, h] = s_new
            o = jnp.sum(cols[:, 0:1] * s_new, axis=0, keepdims=True)
            o_ref[b, :, h * DV_B:(h + 1) * DV_B] = _hgrn_post(o, row(3), w_ref[...])


def _hgrn_sample(hg, lb, gnorm_w, state, tb):
    nb = state.shape[0]
    return pl.pallas_call(
        _hgrn_sample_kernel,
        grid=(nb // tb,),
        in_specs=[pl.BlockSpec((tb, 1, N_HG), lambda i: (i, 0, 0)),
                  pl.BlockSpec((H_B, 1, DK_B), lambda i: (0, 0, 0)),
                  pl.BlockSpec((1, DV_B), lambda i: (0, 0)),
                  pl.BlockSpec((tb, H_B, DK_B, DV_B), lambda i: (i, 0, 0, 0))],
        out_specs=[pl.BlockSpec((tb, 1, H_B * DV_B), lambda i: (i, 0, 0)),
                   pl.BlockSpec((tb, H_B, DK_B, DV_B), lambda i: (i, 0, 0, 0))],
        out_shape=[jax.ShapeDtypeStruct((nb, 1, H_B * DV_B), F32),
                   jax.ShapeDtypeStruct(state.shape, F32)],
        compiler_params=_cparams(("parallel",)),
        name="hgrn_sample",
    )(hg.reshape(nb, 1, N_HG), lb, gnorm_w, state)


def _merge_kernel(x_ref, oa_ref, ob_ref, gt_ref, wa_ref, wb_ref, wo_ref, n2w_ref, wq_ref,
                  keys_ref, h_ref, nt_ref, s1_ref, s2_ref):
    ga = gt_ref[:, :D_MODEL]
    gb = gt_ref[:, D_MODEL:]
    m = (_sigmoid(ga) * _dot(oa_ref[...].astype(BF16), wa_ref[...])
         + _sigmoid(gb) * _dot(ob_ref[...].astype(BF16), wb_ref[...]))
    h = x_ref[...] + _dot(m.astype(BF16), wo_ref[...])
    h_ref[...] = h
    n2 = _rms(h, n2w_ref[...])
    nt_ref[...] = jnp.transpose(n2).astype(BF16)
    qp = _dot(n2.astype(BF16), wq_ref[...]).astype(BF16)
    for hh in range(H_P):
        qh = qp[:, hh * LANES:(hh + 1) * LANES]
        rows = pl.ds(hh, N_KEYS, stride=H_P)
        for c, s_ref in enumerate((s1_ref, s2_ref)):
            s = _dot_nt(keys_ref[c, hh], qh)
            for blk in range(s_ref.shape[0]):
                s_ref[blk, rows, :] = s[:, blk * LANES:(blk + 1) * LANES]


def _merge(x, oa, ob, gt, wa, wb, wo, n2w, wq, keys, tm):
    t = x.shape[0]
    row = lambda n: pl.BlockSpec((tm, n), lambda i: (i, 0))
    full = lambda a: pl.BlockSpec(a.shape, lambda i: (0,) * a.ndim)
    sspec = pl.BlockSpec((tm // LANES, N_KEYS * H_P, LANES), lambda i: (i, 0, 0))
    return pl.pallas_call(
        _merge_kernel,
        grid=(t // tm,),
        in_specs=[row(D_MODEL), row(N_QK), row(H_B * DV_B), row(N_GT),
                  full(wa), full(wb), full(wo), full(n2w), full(wq), full(keys)],
        out_specs=[row(D_MODEL), pl.BlockSpec((D_MODEL, tm), lambda i: (0, i)), sspec, sspec],
        out_shape=[jax.ShapeDtypeStruct((t, D_MODEL), F32),
                   jax.ShapeDtypeStruct((D_MODEL, t), BF16),
                   jax.ShapeDtypeStruct((t // LANES, N_KEYS * H_P, LANES), F32),
                   jax.ShapeDtypeStruct((t // LANES, N_KEYS * H_P, LANES), F32)],
        compiler_params=_cparams(("parallel",)),
        name="merge",
    )(x, oa, ob, gt, wa, wb, wo, n2w, wq, keys)


def _bitonic_merge_desc(a):
    a = list(a)
    d = len(a) // 2
    while d >= 1:
        for i in range(len(a)):
            if i & d == 0:
                a[i], a[i + d] = jnp.maximum(a[i], a[i + d]), jnp.minimum(a[i], a[i + d])
        d //= 2
    return a


def _sort_desc(a):
    if len(a) == 1:
        return list(a)
    half = len(a) // 2
    return _bitonic_merge_desc(_sort_desc(a[:half]) + _sort_desc(a[half:])[::-1])


def _top16(vals):
    groups = [_sort_desc(vals[i:i + TOPK_P]) for i in range(0, len(vals), TOPK_P)]
    while len(groups) > 1:
        groups = [_bitonic_merge_desc([jnp.maximum(a[i], b[TOPK_P - 1 - i])
                                       for i in range(TOPK_P)])
                  for a, b in zip(groups[0::2], groups[1::2])]
    return groups[0]


def _topk_kernel(s1_ref, s2_ref, cnt_ref, e1_ref, r2_ref, e2_ref):
    tl = s1_ref.shape[2]
    key_rows = lambda ref, e: ref[0, pl.ds(e * H_P, H_P), :]
    t1 = _top16([key_rows(s1_ref, e) for e in range(N_KEYS)])
    t2 = _top16([key_rows(s2_ref, e) for e in range(N_KEYS)])
    pairs = [(p, r) for p in range(TOPK_P) for r in range(TOPK_P) if (p + 1) * (r + 1) <= TOPK_P]
    sums = {pr: t1[pr[0]] + t2[pr[1]] for pr in pairs}
    pad = [jnp.full((H_P, tl), NEG, F32)] * (-len(pairs) % (2 * TOPK_P))
    top = _top16([sums[pr] for pr in pairs] + pad)
    tau = top[TOPK_P - 1]
    z = functools.reduce(jnp.add, [jnp.exp(c - top[0]) for c in top])
    theta = []
    for r in range(TOPK_P):
        th = jnp.full((H_P, tl), -NEG, F32)
        for p in range(TOPK_P // (r + 1)):
            th = jnp.where(sums[(p, r)] >= tau, t1[p], th)
        theta.append(th)
    shift1 = t1[0] + jnp.log(z)
    for e in range(N_KEYS):
        x = key_rows(s1_ref, e)
        cnt = jnp.zeros((H_P, tl), F32)
        for r in range(TOPK_P):
            cnt = jnp.where(x >= theta[r], r + 1.0, cnt)
        cnt_ref[pl.ds(e * H_P, H_P), :] = cnt
        e1_ref[pl.ds(e * H_P, H_P), :] = jnp.exp(x - shift1)
    for h in range(H_P):
        xh = s2_ref[0, pl.ds(h, N_KEYS, stride=H_P), :]
        rank = jnp.zeros((N_KEYS, tl), F32)
        for r in range(TOPK_P):
            rank = jnp.where(xh < t2[r][h:h + 1, :], r + 1.0, rank)
        r2_ref[h] = rank.astype(BF16)
        e2_ref[h] = jnp.exp(xh - t2[0][h:h + 1, :]).astype(BF16)


def _topk(s1, s2):
    tl = s1.shape[2]
    t = s1.shape[0] * tl
    sspec = pl.BlockSpec((N_KEYS * H_P, tl), lambda i: (0, i))
    hspec = pl.BlockSpec((H_P, N_KEYS, tl), lambda i: (0, 0, i))
    ispec = pl.BlockSpec((1, N_KEYS * H_P, tl), lambda i: (i, 0, 0))
    return pl.pallas_call(
        _topk_kernel,
        grid=(t // tl,),
        in_specs=[ispec, ispec],
        out_specs=[sspec, sspec, hspec, hspec],
        out_shape=[jax.ShapeDtypeStruct((N_KEYS * H_P, t), F32),
                   jax.ShapeDtypeStruct((N_KEYS * H_P, t), F32),
                   jax.ShapeDtypeStruct((H_P, N_KEYS, t), BF16),
                   jax.ShapeDtypeStruct((H_P, N_KEYS, t), BF16)],
        compiler_params=_cparams(("parallel",)),
        name="topk",
    )(s1, s2)


def _gelu(x):
    return 0.5 * x * (1.0 + lax.erf(x * (2.0 ** -0.5)))


def _peer_kernel(xt_ref, u_ref, vt_ref, cnt_ref, e1_ref, r2_ref, e2_ref, h_ref, nfw_ref,
                 y_ref, acc_ref, *, e1_per_step):
    j = pl.program_id(1)
    nj = pl.num_programs(1)
    tm = xt_ref.shape[1]
    pk = 2 * SUBLANES

    def row_tile(ref, row):
        return jnp.broadcast_to(ref[row, :], (pk, tm)).astype(BF16)[None]

    def weights(e1):
        w = None
        for hh in range(H_P):
            row = pl.ds(e1 * H_P + hh, 1)
            cnt = row_tile(cnt_ref, row)
            e2 = e2_ref[hh].reshape(N_KEYS // pk, pk, tm)
            r2 = r2_ref[hh].reshape(N_KEYS // pk, pk, tm)
            term = jnp.where(r2 < cnt, e2, jnp.zeros_like(e2)) * row_tile(e1_ref, row)
            w = term if w is None else w + term
        return w.reshape(N_KEYS, tm)

    @pl.when(j == 0)
    def _():
        acc_ref[...] = jnp.zeros(acc_ref.shape, F32)

    w = jnp.concatenate([weights(j * e1_per_step + a) for a in range(e1_per_step)], axis=0)
    act = _gelu(_dot(u_ref[...], xt_ref[...]))
    acc_ref[...] += _dot(vt_ref[...], w * act.astype(BF16))

    @pl.when(j == nj - 1)
    def _():
        y_ref[...] = _rms(h_ref[...] + jnp.transpose(acc_ref[...]), nfw_ref[...])


def _peer(xt, u, v, cnt, e1w, r2, e2w, h, nfw, tm, e1_per_step):
    t = h.shape[0]
    te = e1_per_step * N_KEYS
    kspec = pl.BlockSpec((N_KEYS * H_P, tm), lambda i, j: (0, i))
    hspec = pl.BlockSpec((H_P, N_KEYS, tm), lambda i, j: (0, 0, i))
    row = pl.BlockSpec((tm, D_MODEL), lambda i, j: (i, 0))
    return pl.pallas_call(
        functools.partial(_peer_kernel, e1_per_step=e1_per_step),
        grid=(t // tm, N_KEYS // e1_per_step),
        in_specs=[pl.BlockSpec((D_MODEL, tm), lambda i, j: (0, i)),
                  pl.BlockSpec((te, D_MODEL), lambda i, j: (j, 0)),
                  pl.BlockSpec((D_MODEL, te), lambda i, j: (0, j)),
                  kspec, kspec, hspec, hspec, row,
                  pl.BlockSpec((1, D_MODEL), lambda i, j: (0, 0))],
        out_specs=row,
        out_shape=jax.ShapeDtypeStruct((t, D_MODEL), F32),
        scratch_shapes=[pltpu.VMEM((D_MODEL, tm), F32)],
        compiler_params=_cparams(("parallel", "arbitrary")),
        name="peer",
    )(xt, u, v, cnt, e1w, r2, e2w, h, nfw)


def _tile(n, pref):
    t = min(n, pref)
    assert n % t == 0, (n, t)
    return t


def kernel(x_prompt, x_sample, cache_k, cache_v, state_hgrn, page_table, norm1_w, w_in,
           lambda_q1, lambda_k1, lambda_q2, lambda_k2, subln_w, lb_param, gnorm_w,
           w_branch_a, w_branch_b, w_out, norm2_w, w_query, sub_keys, expert_u, expert_v,
           norm_f_w):
    depth = w_in.shape[0]
    assert depth == 1 and w_in.shape[2] == N_IN
    batch, seq, _ = x_prompt.shape
    n_dec, dec_len, _ = x_sample.shape
    assert dec_len == 1
    l = 0

    lam_init = 0.8 - 0.6 * math.exp(-0.3 * l)
    post_scale = 1.0 - lam_init
    f = F32
    lam = (jnp.exp(jnp.sum(lambda_q1[l].astype(f) * lambda_k1[l].astype(f)))
           - jnp.exp(jnp.sum(lambda_q2[l].astype(f) * lambda_k2[l].astype(f)))
           + lam_init).reshape(1)
    lb = jnp.cumsum(jax.nn.softmax(lb_param.astype(f), axis=0), axis=0)[l].reshape(H_B, 1, DK_B)

    n1w = norm1_w[l].reshape(1, D_MODEL)
    n2w = norm2_w[l].reshape(1, D_MODEL)
    nfw = norm_f_w.reshape(1, D_MODEL)
    sub_w = subln_w[l].reshape(1, DV_A)
    gn_w = gnorm_w[l].reshape(1, DV_B)
    win = w_in[l].astype(BF16)
    wa = w_branch_a[l].astype(BF16)
    wb = w_branch_b[l].astype(BF16)
    wo = w_out[l].astype(BF16)
    wq = w_query[l].astype(BF16)
    sk = jnp.transpose(sub_keys[l].astype(BF16), (1, 0, 2, 3))
    zk = jnp.zeros_like(sk[0])
    keys = jnp.stack([jnp.concatenate([sk[0], zk], axis=-1),
                      jnp.concatenate([zk, sk[1]], axis=-1)])
    u = expert_u.reshape(expert_u.shape[1:]).astype(BF16)
    v = jnp.transpose(expert_v.reshape(expert_v.shape[1:])).astype(BF16)

    def tail(x2, oa, ob, gt, tm_merge, tm_peer, e1_per_step=16):
        h, nt, s1, s2 = _merge(x2, oa, ob, gt, wa, wb, wo, n2w, wq, keys, tm_merge)
        cnt, e1w, r2, e2w = _topk(s1, s2)
        return _peer(nt, u, v, cnt, e1w, r2, e2w, h, nfw, tm_peer, e1_per_step)


    def keys_out(kt, b, s):
        return jnp.transpose(kt.reshape(1, b, H_A, 2, DH_QK, s), (0, 1, 5, 2, 3, 4))

    t = batch * seq
    xp = x_prompt.reshape(t, D_MODEL)
    qa, ka, kt, va, vat, hg, gt = _in_proj(xp, n1w, win, batch, _tile(seq, 256))
    tq = _tile(seq, 1024)
    oa = _attn_prompt(lam, qa, ka, vat, sub_w, batch, seq, tq, _tile(tq, 1024), post_scale)
    ob, s_p = _hgrn_prompt(hg, lb, gn_w, batch, seq, _tile(seq, 128), _tile(seq, 1024))
    y_p = tail(xp, oa, ob, gt, _tile(t, 256), _tile(t, 512))

    xs = x_sample.reshape(n_dec, D_MODEL)
    qa_s, ka_s, kt_s, va_s, _, hg_s, gt_s = _in_proj(xs, n1w, win, 1, _tile(n_dec, 128))
    n_phys, page = cache_k.shape[1], cache_k.shape[2]
    cache_kt = jnp.transpose(cache_k, (0, 1, 3, 4, 5, 2)).reshape(n_phys, N_QK, page)
    cache_v2 = cache_v.reshape(n_phys, page * H_A, DV_A)
    oa_s = _attn_decode(page_table, lam, qa_s, ka_s, va_s, sub_w, cache_kt, cache_v2, post_scale)
    ob_s, s_s = _hgrn_sample(hg_s, lb, gn_w, state_hgrn.reshape(state_hgrn.shape[1:]),
                             _tile(n_dec, 8))
    y_s = tail(xs, oa_s, ob_s.reshape(n_dec, H_B * DV_B), gt_s,
               _tile(n_dec, 128), _tile(n_dec, 128))

    return (y_p.reshape(batch, seq, D_MODEL),
            y_s.reshape(n_dec, 1, D_MODEL),
            keys_out(kt, batch, seq),
            va.reshape(1, batch, seq, H_A, DV_A),
            s_p.reshape(1, batch, H_B, DK_B, DV_B),
            jnp.transpose(keys_out(kt_s, 1, n_dec), (0, 2, 1, 3, 4, 5)),
            va_s.reshape(1, n_dec, 1, H_A, DV_A),
            s_s.reshape(1, n_dec, H_B, DK_B, DV_B))
```

```python
import functools
import math

import jax
import jax.numpy as jnp
from jax import lax
from jax.experimental import pallas as pl
from jax.experimental.pallas import tpu as pltpu

F32 = jnp.float32
BF16 = jnp.bfloat16

EPS = 1e-6
NEG = -1e30
LOG2E = math.log2(math.e)
LANES = 128
SUBLANES = 8
VMEM_LIMIT = 56 * 1024 * 1024

H_A = 4
DH_QK = 64
DV_A = 128
H_B = 4
DK_B = 128
DV_B = 128
N_KEYS = 128
H_P = 8
DK_P_HALF = 64
TOPK_P = 16
D_MODEL = 1024
N_QK = H_A * 2 * DH_QK
N_HG = 4 * H_B * DK_B
N_GT = 2 * D_MODEL
N_IN = 3 * N_QK + N_HG + N_GT


def _cparams(sem):
    return pltpu.CompilerParams(dimension_semantics=sem, vmem_limit_bytes=VMEM_LIMIT)


def _rms(x, w):
    return x * lax.rsqrt(jnp.mean(x * x, axis=-1, keepdims=True) + EPS) * w


def _sigmoid(x):
    return 1.0 / (1.0 + jnp.exp(-x))


def _silu(x):
    return x * _sigmoid(x)


def _dot(a, b):
    return jnp.dot(a, b, preferred_element_type=F32)


def _dot_nt(a, b):
    return lax.dot_general(a, b, (((1,), (1,)), ((), ())), preferred_element_type=F32)


def _inproj_kernel(x_ref, g_ref, w_ref, qa_ref, ka_ref, kt_ref, va_ref, vt_ref,
                   hg_ref, gt_ref):
    nb = _rms(x_ref[...], g_ref[...]).astype(BF16)
    cw = N_QK

    def mm(c):
        return _dot(nb, w_ref[:, c * cw:(c + 1) * cw])

    qa_ref[...] = (mm(0) * (DH_QK ** -0.5 * LOG2E)).astype(BF16)
    ka = mm(1)
    ka_ref[...] = ka.astype(BF16)
    kt_ref[0] = jnp.transpose(ka)
    va = mm(2)
    for h in range(H_A):
        va_ref[pl.ds(h, va.shape[0], stride=H_A), :] = va[:, h * DV_A:(h + 1) * DV_A]
    vt_ref[0] = jnp.transpose(va).astype(BF16)
    for c in range(N_HG // cw):
        hg_ref[:, c * cw:(c + 1) * cw] = mm(3 + c)
    for c in range(N_GT // cw):
        gt_ref[:, c * cw:(c + 1) * cw] = mm(3 + N_HG // cw + c)


def _in_proj(x, g, w, batch, tm):
    m = x.shape[0]
    seq = m // batch
    nt = seq // tm
    row = lambda n: pl.BlockSpec((tm, n), lambda b, i: (b * nt + i, 0))
    full = lambda a: pl.BlockSpec(a.shape, lambda b, i: (0, 0))
    tspec = pl.BlockSpec((1, N_QK, tm), lambda b, i: (b, 0, i))
    return pl.pallas_call(
        _inproj_kernel,
        grid=(batch, nt),
        in_specs=[row(D_MODEL), full(g), full(w)],
        out_specs=[row(N_QK), row(N_QK), tspec,
                   pl.BlockSpec((tm * H_A, DV_A), lambda b, i: (b * nt + i, 0)),
                   tspec, row(N_HG), row(N_GT)],
        out_shape=[jax.ShapeDtypeStruct((m, N_QK), BF16),
                   jax.ShapeDtypeStruct((m, N_QK), BF16),
                   jax.ShapeDtypeStruct((batch, N_QK, seq), F32),
                   jax.ShapeDtypeStruct((m * H_A, DV_A), F32),
                   jax.ShapeDtypeStruct((batch, N_QK, seq), BF16),
                   jax.ShapeDtypeStruct((m, N_HG), F32),
                   jax.ShapeDtypeStruct((m, N_GT), F32)],
        compiler_params=_cparams(("parallel", "parallel")),
        name="in_proj",
    )(x, g, w)


def _attn_kernel(lam_ref, q_ref, k_ref, vt_ref, w_ref, o_ref,
                 m1, l1, a1, m2, l2, a2, *, tk, post_scale):
    qi = pl.program_id(2)
    tq = q_ref.shape[0]

    for m, l, a in ((m1, l1, a1), (m2, l2, a2)):
        m[...] = jnp.full(m.shape, NEG, F32)
        l[...] = jnp.zeros(l.shape, F32)
        a[...] = jnp.zeros(a.shape, F32)

    q = q_ref[...]
    lane = lax.broadcasted_iota(jnp.int32, q.shape, 1)
    qcs = [jnp.where((lane >= DH_QK) == bool(c), q, jnp.zeros_like(q)) for c in range(2)]

    def step(start, diag_offset):
        start = pl.multiple_of(start, tk)
        k = k_ref[pl.ds(start, tk), :]
        vt = vt_ref[0, :, pl.ds(start, tk)]
        if diag_offset is not None:
            krow = lax.broadcasted_iota(jnp.int32, (tk, tq), 0) + diag_offset
            qcol = lax.broadcasted_iota(jnp.int32, (tk, tq), 1)
            keep = krow <= qcol
        for qc, (m, l, a) in zip(qcs, ((m1, l1, a1), (m2, l2, a2))):
            s = _dot_nt(k, qc)
            if diag_offset is not None:
                s = jnp.where(keep, s, NEG)
            m_prev = m[...]
            m_new = jnp.maximum(m_prev, jnp.max(s, axis=0, keepdims=True))
            alpha = jnp.exp2(m_prev - m_new)
            p = jnp.exp2(s - m_new)
            l[...] = alpha * l[...] + jnp.sum(p, axis=0, keepdims=True)
            a[...] = alpha * a[...] + _dot(vt, p.astype(BF16))
            m[...] = m_new

    def body(ki, carry):
        step(ki * tk, None)
        return carry

    lax.fori_loop(0, qi * (tq // tk), body, 0)
    for j in range(tq // tk):
        step(qi * tq + j * tk, j * tk)

    lam = lam_ref[0]
    ot = a1[...] / l1[...] - lam * (a2[...] / l2[...])
    o_ref[...] = _rms(jnp.transpose(ot), w_ref[...]) * post_scale


def _attn_prompt(lam, qa, ka, vat, subln_w, batch, seq, tq, tk, post_scale):
    nq = seq // tq
    return pl.pallas_call(
        functools.partial(_attn_kernel, tk=tk, post_scale=post_scale),
        grid=(batch, H_A, nq),
        in_specs=[pl.BlockSpec(memory_space=pltpu.SMEM),
                  pl.BlockSpec((tq, DV_A), lambda b, h, qi: (b * nq + qi, h)),
                  pl.BlockSpec((seq, DV_A), lambda b, h, qi: (b, h)),
                  pl.BlockSpec((1, DV_A, seq), lambda b, h, qi: (b, h, 0)),
                  pl.BlockSpec((1, DV_A), lambda b, h, qi: (0, 0))],
        out_specs=pl.BlockSpec((tq, DV_A), lambda b, h, qi: (b * nq + qi, h)),
        out_shape=jax.ShapeDtypeStruct((batch * seq, H_A * DV_A), F32),
        scratch_shapes=[pltpu.VMEM((1, tq), F32), pltpu.VMEM((1, tq), F32),
                        pltpu.VMEM((DV_A, tq), F32)] * 2,
        compiler_params=_cparams(("parallel", "parallel", "arbitrary")),
        name="attn_prompt",
    )(lam, qa, ka, vat, subln_w)


def _decode_kernel(pt_ref, lam_ref, q_ref, kn_ref, vn_ref, w_ref, *refs, n_pages, post_scale):
    k_refs = refs[:n_pages]
    v_refs = refs[n_pages:2 * n_pages]
    o_ref = refs[2 * n_pages]
    del pt_ref
    nmap = 2 * H_A
    width = N_QK
    page = k_refs[0].shape[2]

    q = q_ref[0].astype(F32)
    rows = lax.broadcasted_iota(jnp.int32, (nmap, width), 0)
    lane = lax.broadcasted_iota(jnp.int32, (nmap, width), 1)
    qrows = jnp.where(lane // DH_QK == rows, jnp.broadcast_to(q, (nmap, width)), 0.0)
    qb = qrows.astype(BF16)

    s_new = jnp.sum(qrows * kn_ref[0].astype(F32), axis=-1, keepdims=True)
    s = [_dot(qb, k_refs[i][0].astype(BF16)) for i in range(n_pages)]
    m = functools.reduce(jnp.maximum, s)
    m = jnp.maximum(jnp.max(m, axis=-1, keepdims=True), s_new)
    p = [jnp.exp2(si - m) for si in s]
    p_new = jnp.exp2(s_new - m)
    l = jnp.sum(functools.reduce(jnp.add, p), axis=-1, keepdims=True) + p_new

    r8 = lax.broadcasted_iota(jnp.int32, (nmap, page), 0)
    o8 = jnp.zeros((nmap, DV_A), F32)
    for i in range(n_pages):
        pb = p[i].astype(BF16)
        for h in range(H_A):
            vh = v_refs[i][0, pl.ds(h, page, stride=H_A), :].astype(BF16)
            o8 = o8 + _dot(jnp.where(r8 // 2 == h, pb, jnp.zeros_like(pb)), vh)

    vn = vn_ref[0].astype(BF16).astype(F32)
    r8v = lax.broadcasted_iota(jnp.int32, (nmap, DV_A), 0)
    vn8 = jnp.zeros((nmap, DV_A), F32)
    for h in range(H_A):
        vn8 = jnp.where(r8v // 2 == h, jnp.broadcast_to(vn[h:h + 1, :], (nmap, DV_A)), vn8)
    o8 = (o8 + p_new.astype(BF16).astype(F32) * vn8) / l
    lam = lam_ref[0]
    w = w_ref[...]
    for h in range(H_A):
        d = o8[2 * h:2 * h + 1] - lam * o8[2 * h + 1:2 * h + 2]
        o_ref[0, :, h * DV_A:(h + 1) * DV_A] = _rms(d, w) * post_scale


def _attn_decode(page_table, lam, qa, ka, va, subln_w, cache_kt, cache_v, post_scale):
    nb, n_pages = page_table.shape
    width, page = cache_kt.shape[1], cache_kt.shape[2]
    one = lambda: pl.BlockSpec((1, 1, width), lambda b, pt: (b, 0, 0))
    kspec = lambda i: pl.BlockSpec((1, width, page), lambda b, pt: (pt[b, i], 0, 0))
    vspec = lambda i: pl.BlockSpec((1, page * H_A, DV_A), lambda b, pt: (pt[b, i], 0, 0))
    grid_spec = pltpu.PrefetchScalarGridSpec(
        num_scalar_prefetch=1,
        grid=(nb,),
        in_specs=[pl.BlockSpec(memory_space=pltpu.SMEM), one(), one(),
                  pl.BlockSpec((1, H_A, DV_A), lambda b, pt: (b, 0, 0)),
                  pl.BlockSpec((1, DV_A), lambda b, pt: (0, 0))]
                 + [kspec(i) for i in range(n_pages)] + [vspec(i) for i in range(n_pages)],
        out_specs=one(),
    )
    r3 = lambda a: a.reshape(nb, 1, width)
    out = pl.pallas_call(
        functools.partial(_decode_kernel, n_pages=n_pages, post_scale=post_scale),
        grid_spec=grid_spec,
        out_shape=jax.ShapeDtypeStruct((nb, 1, width), F32),
        compiler_params=_cparams(("arbitrary",)),
        name="attn_decode",
    )(page_table, lam, r3(qa), r3(ka), va.reshape(nb, H_A, DV_A), subln_w,
      *([cache_kt] * n_pages), *([cache_v] * n_pages))
    return out.reshape(nb, width)


def _hgrn_gates(qb, fb, lb):
    q = _silu(qb) * (DK_B ** -0.5)
    f = lb + (1.0 - lb) * _sigmoid(fb)
    return q, 1.0 - f, jnp.log(f)


def _hgrn_post(o, gb, w):
    return _rms(o, w) * _silu(gb)


def _hgrn_prompt_kernel(qb_ref, fb_ref, ib_ref, gb_ref, lb_ref, w_ref, o_ref, s_ref, st_ref, *,
                        chunk):
    c = chunk
    dk = DK_B
    seq = qb_ref.shape[0]
    heads = lb_ref.shape[0]
    gw = w_ref[...]

    row = lax.broadcasted_iota(jnp.int32, (c, c), 0)
    col = lax.broadcasted_iota(jnp.int32, (c, c), 1)
    tril = jnp.where(col <= row, 1.0, 0.0).astype(BF16)
    ones = jnp.ones((dk, c), BF16)
    levels = []
    m = SUBLANES
    while 2 * m <= c:
        levels.append((m, jnp.where(row // (2 * m) == col // (2 * m), 1.0, 0.0)))
        m *= 2
    diag = [jnp.where(col == (row // SUBLANES) * SUBLANES + s, 1.0, 0.0) for s in range(SUBLANES)]

    def one_head(ci, st, hh):
        rows = pl.ds(pl.multiple_of(ci * c, c), c)
        lanes = slice(hh * dk, (hh + 1) * dk)
        q, k, lf = _hgrn_gates(qb_ref[rows, lanes], fb_ref[rows, lanes], lb_ref[hh])
        v = ib_ref[rows, lanes]

        hi = lf.astype(BF16)
        r1 = lf - hi.astype(F32)
        mid = r1.astype(BF16)
        lo = (r1 - mid.astype(F32)).astype(BF16)
        g = _dot(tril, hi) + _dot(tril, mid) + _dot(tril, lo)

        o = _dot_nt((q * jnp.exp(g)).astype(BF16), st.astype(BF16))

        a = jnp.zeros((c, c), F32)
        for m, same_block in levels:
            nb = c // (2 * m)
            g3 = g.reshape(nb, 2 * m, dk)
            d = g3 - g3[:, m - 1:m, :]
            rin = lax.broadcasted_iota(jnp.int32, (nb, 2 * m, dk), 1)
            qs = q.reshape(nb, 2 * m, dk) * jnp.exp(jnp.where(rin >= m, d, NEG))
            ks = k.reshape(nb, 2 * m, dk) * jnp.exp(jnp.where(rin < m, -d, NEG))
            al = _dot_nt(qs.reshape(c, dk).astype(BF16), ks.reshape(c, dk).astype(BF16))
            a = a + al * same_block

        nb = c // SUBLANES
        g8 = g.reshape(nb, SUBLANES, dk)
        q8 = q.reshape(nb, SUBLANES, dk)
        k8 = k.reshape(nb, SUBLANES, dk)
        rin = lax.broadcasted_iota(jnp.int32, (nb, SUBLANES, dk), 1)
        for s in range(SUBLANES):
            d = g8 - g8[:, s:s + 1, :]
            p = q8 * k8[:, s:s + 1, :] * jnp.exp(jnp.where(rin >= s, d, NEG))
            r = _dot(p.reshape(c, dk).astype(BF16), ones)
            a = a + r * diag[s]

        o = o + _dot(a.astype(BF16), v.astype(BF16))
        o_ref[rows, lanes] = _hgrn_post(o, gb_ref[rows, lanes], gw)

        g_end = g[c - 1:c, :]
        kd = (k * jnp.exp(g_end - g)).astype(BF16)
        return st * jnp.exp(g_end) + _dot(jnp.transpose(v).astype(BF16), kd)

    def body(ci, sts):
        return tuple(one_head(ci, st, hh) for hh, st in enumerate(sts))

    sb = pl.program_id(1)

    @pl.when(sb == 0)
    def _():
        st_ref[...] = jnp.zeros(st_ref.shape, F32)

    sts = lax.fori_loop(0, seq // c, body, tuple(st_ref[hh] for hh in range(heads)))
    for hh, st in enumerate(sts):
        st_ref[hh] = st

    @pl.when(sb == pl.num_programs(1) - 1)
    def _():
        for hh, st in enumerate(sts):
            s_ref[0, hh] = jnp.transpose(st)


def _hgrn_prompt(hg, lb, gnorm_w, batch, seq, chunk, rows):
    nsb = seq // rows
    width = H_B * DK_B
    blk = lambda j: pl.BlockSpec((rows, width), lambda b, s: (b * nsb + s, j))
    return pl.pallas_call(
        functools.partial(_hgrn_prompt_kernel, chunk=chunk),
        grid=(batch, nsb),
        in_specs=[blk(0), blk(1), blk(2), blk(3),
                  pl.BlockSpec((H_B, 1, DK_B), lambda b, s: (0, 0, 0)),
                  pl.BlockSpec((1, DV_B), lambda b, s: (0, 0))],
        out_specs=[pl.BlockSpec((rows, H_B * DV_B), lambda b, s: (b * nsb + s, 0)),
                   pl.BlockSpec((1, H_B, DK_B, DV_B), lambda b, s: (b, 0, 0, 0))],
        out_shape=[jax.ShapeDtypeStruct((batch * seq, H_B * DV_B), F32),
                   jax.ShapeDtypeStruct((batch, H_B, DK_B, DV_B), F32)],
        scratch_shapes=[pltpu.VMEM((H_B, DV_B, DK_B), F32)],
        compiler_params=_cparams(("parallel", "arbitrary")),
        name="hgrn_prompt",
    )(hg, hg, hg, hg, lb, gnorm_w)


def _hgrn_sample_kernel(hg_ref, lb_ref, w_ref, s0_ref, o_ref, s_ref):
    nb = s0_ref.shape[0]
    for b in range(nb):
        for h in range(H_B):
            sl = lambda j: slice((j * H_B + h) * DK_B, (j * H_B + h + 1) * DK_B)
            row = lambda j: hg_ref[b, :, sl(j)]
            q, k, lf = _hgrn_gates(row(0), row(1), lb_ref[h])
            v = row(2)
            stack = jnp.concatenate(
                [q, k, jnp.exp(lf), jnp.zeros((DK_B - 3, DK_B), F32)], axis=0)
            cols = jnp.transpose(stack)
            s_new = cols[:, 2:3] * s0_ref[b, h] + cols[:, 1:2] * v
            s_ref[b, h] = s_new
            o = jnp.sum(cols[:, 0:1] * s_new, axis=0, keepdims=True)
            o_ref[b, :, h * DV_B:(h + 1) * DV_B] = _hgrn_post(o, row(3), w_ref[...])


def _hgrn_sample(hg, lb, gnorm_w, state, tb):
    nb = state.shape[0]
    return pl.pallas_call(
        _hgrn_sample_kernel,
        grid=(nb // tb,),
        in_specs=[pl.BlockSpec((tb, 1, N_HG), lambda i: (i, 0, 0)),
                  pl.BlockSpec((H_B, 1, DK_B), lambda i: (0, 0, 0)),
                  pl.BlockSpec((1, DV_B), lambda i: (0, 0)),
                  pl.BlockSpec((tb, H_B, DK_B, DV_B), lambda i: (i, 0, 0, 0))],
        out_specs=[pl.BlockSpec((tb, 1, H_B * DV_B), lambda i: (i, 0, 0)),
                   pl.BlockSpec((tb, H_B, DK_B, DV_B), lambda i: (i, 0, 0, 0))],
        out_shape=[jax.ShapeDtypeStruct((nb, 1, H_B * DV_B), F32),
                   jax.ShapeDtypeStruct(state.shape, F32)],
        compiler_params=_cparams(("parallel",)),
        name="hgrn_sample",
    )(hg.reshape(nb, 1, N_HG), lb, gnorm_w, state)


def _merge_kernel(x_ref, oa_ref, ob_ref, gt_ref, wa_ref, wb_ref, wo_ref, n2w_ref, wq_ref,
                  keys_ref, h_ref, nt_ref, s1_ref, s2_ref):
    ga = gt_ref[:, :D_MODEL]
    gb = gt_ref[:, D_MODEL:]
    m = (_sigmoid(ga) * _dot(oa_ref[...].astype(BF16), wa_ref[...])
         + _sigmoid(gb) * _dot(ob_ref[...].astype(BF16), wb_ref[...]))
    h = x_ref[...] + _dot(m.astype(BF16), wo_ref[...])
    h_ref[...] = h
    n2 = _rms(h, n2w_ref[...])
    nt_ref[...] = jnp.transpose(n2).astype(BF16)
    qp = _dot(n2.astype(BF16), wq_ref[...]).astype(BF16)
    for hh in range(H_P):
        qh = qp[:, hh * LANES:(hh + 1) * LANES]
        rows = pl.ds(hh, N_KEYS, stride=H_P)
        for c, s_ref in enumerate((s1_ref, s2_ref)):
            s = _dot_nt(keys_ref[c, hh], qh)
            for blk in range(s_ref.shape[0]):
                s_ref[blk, rows, :] = s[:, blk * LANES:(blk + 1) * LANES]


def _merge(x, oa, ob, gt, wa, wb, wo, n2w, wq, keys, tm):
    t = x.shape[0]
    row = lambda n: pl.BlockSpec((tm, n), lambda i: (i, 0))
    full = lambda a: pl.BlockSpec(a.shape, lambda i: (0,) * a.ndim)
    sspec = pl.BlockSpec((tm // LANES, N_KEYS * H_P, LANES), lambda i: (i, 0, 0))
    return pl.pallas_call(
        _merge_kernel,
        grid=(t // tm,),
        in_specs=[row(D_MODEL), row(N_QK), row(H_B * DV_B), row(N_GT),
                  full(wa), full(wb), full(wo), full(n2w), full(wq), full(keys)],
        out_specs=[row(D_MODEL), pl.BlockSpec((D_MODEL, tm), lambda i: (0, i)), sspec, sspec],
        out_shape=[jax.ShapeDtypeStruct((t, D_MODEL), F32),
                   jax.ShapeDtypeStruct((D_MODEL, t), BF16),
                   jax.ShapeDtypeStruct((t // LANES, N_KEYS * H_P, LANES), F32),
                   jax.ShapeDtypeStruct((t // LANES, N_KEYS * H_P, LANES), F32)],
        compiler_params=_cparams(("parallel",)),
        name="merge",
    )(x, oa, ob, gt, wa, wb, wo, n2w, wq, keys)


def _bitonic_merge_desc(a):
    a = list(a)
    d = len(a) // 2
    while d >= 1:
        for i in range(len(a)):
            if i & d == 0:
                a[i], a[i + d] = jnp.maximum(a[i], a[i + d]), jnp.minimum(a[i], a[i + d])
        d //= 2
    return a


def _sort_desc(a):
    if len(a) == 1:
        return list(a)
    half = len(a) // 2
    return _bitonic_merge_desc(_sort_desc(a[:half]) + _sort_desc(a[half:])[::-1])


def _top16(vals):
    groups = [_sort_desc(vals[i:i + TOPK_P]) for i in range(0, len(vals), TOPK_P)]
    while len(groups) > 1:
        groups = [_bitonic_merge_desc([jnp.maximum(a[i], b[TOPK_P - 1 - i])
                                       for i in range(TOPK_P)])
                  for a, b in zip(groups[0::2], groups[1::2])]
    return groups[0]


def _prefix_count(x, th, below):
    assert len(th) == TOPK_P == 16
    hit = (lambda t: x < t) if below else (lambda t: x >= t)
    b3 = hit(th[7])
    b2 = hit(jnp.where(b3, th[11], th[3]))
    b1 = hit(jnp.where(b3, jnp.where(b2, th[13], th[9]), jnp.where(b2, th[5], th[1])))
    lo = jnp.where(b2, jnp.where(b1, th[6], th[4]), jnp.where(b1, th[2], th[0]))
    hi = jnp.where(b2, jnp.where(b1, th[14], th[12]), jnp.where(b1, th[10], th[8]))
    b0 = hit(jnp.where(b3, hi, lo))
    bit = lambda b, v: jnp.where(b, v, 0.0)
    return bit(b3, 8.0) + bit(b2, 4.0) + bit(b1, 2.0) + bit(b0, 1.0) + bit(hit(th[15]), 1.0)


def _topk_kernel(s1_ref, s2_ref, cnt_ref, e1_ref, r2_ref, e2_ref):
    tl = s1_ref.shape[2]
    key_rows = lambda ref, e: ref[0, pl.ds(e * H_P, H_P), :]
    t1 = _top16([key_rows(s1_ref, e) for e in range(N_KEYS)])
    t2 = _top16([key_rows(s2_ref, e) for e in range(N_KEYS)])
    pairs = [(p, r) for p in range(TOPK_P) for r in range(TOPK_P) if (p + 1) * (r + 1) <= TOPK_P]
    sums = {pr: t1[pr[0]] + t2[pr[1]] for pr in pairs}
    pad = [jnp.full((H_P, tl), NEG, F32)] * (-len(pairs) % (2 * TOPK_P))
    top = _top16([sums[pr] for pr in pairs] + pad)
    tau = top[TOPK_P - 1]
    z = functools.reduce(jnp.add, [jnp.exp(c - top[0]) for c in top])
    theta = []
    for r in range(TOPK_P):
        th = jnp.full((H_P, tl), -NEG, F32)
        for p in range(TOPK_P // (r + 1)):
            th = jnp.where(sums[(p, r)] >= tau, t1[p], th)
        theta.append(th)
    shift1 = t1[0] + jnp.log(z)
    for e in range(N_KEYS):
        x = key_rows(s1_ref, e)
        cnt_ref[pl.ds(e * H_P, H_P), :] = _prefix_count(x, theta, below=False)
        e1_ref[pl.ds(e * H_P, H_P), :] = jnp.exp(x - shift1)
    for h in range(H_P):
        xh = s2_ref[0, pl.ds(h, N_KEYS, stride=H_P), :]
        rank = _prefix_count(xh, [t[h:h + 1, :] for t in t2], below=True)
        r2_ref[h] = rank.astype(BF16)
        e2_ref[h] = jnp.exp(xh - t2[0][h:h + 1, :]).astype(BF16)


def _topk(s1, s2):
    tl = s1.shape[2]
    t = s1.shape[0] * tl
    sspec = pl.BlockSpec((N_KEYS * H_P, tl), lambda i: (0, i))
    hspec = pl.BlockSpec((H_P, N_KEYS, tl), lambda i: (0, 0, i))
    ispec = pl.BlockSpec((1, N_KEYS * H_P, tl), lambda i: (i, 0, 0))
    return pl.pallas_call(
        _topk_kernel,
        grid=(t // tl,),
        in_specs=[ispec, ispec],
        out_specs=[sspec, sspec, hspec, hspec],
        out_shape=[jax.ShapeDtypeStruct((N_KEYS * H_P, t), F32),
                   jax.ShapeDtypeStruct((N_KEYS * H_P, t), F32),
                   jax.ShapeDtypeStruct((H_P, N_KEYS, t), BF16),
                   jax.ShapeDtypeStruct((H_P, N_KEYS, t), BF16)],
        compiler_params=_cparams(("parallel",)),
        name="topk",
    )(s1, s2)


def _gelu(x):
    return 0.5 * x * (1.0 + lax.erf(x * (2.0 ** -0.5)))


def _peer_kernel(xt_ref, u_ref, vt_ref, cnt_ref, e1_ref, r2_ref, e2_ref, h_ref, nfw_ref,
                 y_ref, acc_ref, *, e1_per_step):
    j = pl.program_id(1)
    nj = pl.num_programs(1)
    tm = xt_ref.shape[1]
    pk = 2 * SUBLANES

    def row_tile(ref, row):
        return jnp.broadcast_to(ref[row, :], (pk, tm)).astype(BF16)[None]

    def weights(e1):
        w = None
        for hh in range(H_P):
            row = pl.ds(e1 * H_P + hh, 1)
            cnt = row_tile(cnt_ref, row)
            e2 = e2_ref[hh].reshape(N_KEYS // pk, pk, tm)
            r2 = r2_ref[hh].reshape(N_KEYS // pk, pk, tm)
            term = jnp.where(r2 < cnt, e2, jnp.zeros_like(e2)) * row_tile(e1_ref, row)
            w = term if w is None else w + term
        return w.reshape(N_KEYS, tm)

    @pl.when(j == 0)
    def _():
        acc_ref[...] = jnp.zeros(acc_ref.shape, F32)

    w = jnp.concatenate([weights(j * e1_per_step + a) for a in range(e1_per_step)], axis=0)
    act = _gelu(_dot(u_ref[...], xt_ref[...]))
    acc_ref[...] += _dot(vt_ref[...], w * act.astype(BF16))

    @pl.when(j == nj - 1)
    def _():
        y_ref[...] = _rms(h_ref[...] + jnp.transpose(acc_ref[...]), nfw_ref[...])


def _peer(xt, u, v, cnt, e1w, r2, e2w, h, nfw, tm, e1_per_step):
    t = h.shape[0]
    te = e1_per_step * N_KEYS
    kspec = pl.BlockSpec((N_KEYS * H_P, tm), lambda i, j: (0, i))
    hspec = pl.BlockSpec((H_P, N_KEYS, tm), lambda i, j: (0, 0, i))
    row = pl.BlockSpec((tm, D_MODEL), lambda i, j: (i, 0))
    return pl.pallas_call(
        functools.partial(_peer_kernel, e1_per_step=e1_per_step),
        grid=(t // tm, N_KEYS // e1_per_step),
        in_specs=[pl.BlockSpec((D_MODEL, tm), lambda i, j: (0, i)),
                  pl.BlockSpec((te, D_MODEL), lambda i, j: (j, 0)),
                  pl.BlockSpec((D_MODEL, te), lambda i, j: (0, j)),
                  kspec, kspec, hspec, hspec, row,
                  pl.BlockSpec((1, D_MODEL), lambda i, j: (0, 0))],
        out_specs=row,
        out_shape=jax.ShapeDtypeStruct((t, D_MODEL), F32),
        scratch_shapes=[pltpu.VMEM((D_MODEL, tm), F32)],
        compiler_params=_cparams(("parallel", "arbitrary")),
        name="peer",
    )(xt, u, v, cnt, e1w, r2, e2w, h, nfw)


def _tile(n, pref):
    t = min(n, pref)
    assert n % t == 0, (n, t)
    return t


def kernel(x_prompt, x_sample, cache_k, cache_v, state_hgrn, page_table, norm1_w, w_in,
           lambda_q1, lambda_k1, lambda_q2, lambda_k2, subln_w, lb_param, gnorm_w,
           w_branch_a, w_branch_b, w_out, norm2_w, w_query, sub_keys, expert_u, expert_v,
           norm_f_w):
    depth = w_in.shape[0]
    assert depth == 1 and w_in.shape[2] == N_IN
    batch, seq, _ = x_prompt.shape
    n_dec, dec_len, _ = x_sample.shape
    assert dec_len == 1
    l = 0

    lam_init = 0.8 - 0.6 * math.exp(-0.3 * l)
    post_scale = 1.0 - lam_init
    f = F32
    lam = (jnp.exp(jnp.sum(lambda_q1[l].astype(f) * lambda_k1[l].astype(f)))
           - jnp.exp(jnp.sum(lambda_q2[l].astype(f) * lambda_k2[l].astype(f)))
           + lam_init).reshape(1)
    lb = jnp.cumsum(jax.nn.softmax(lb_param.astype(f), axis=0), axis=0)[l].reshape(H_B, 1, DK_B)

    n1w = norm1_w[l].reshape(1, D_MODEL)
    n2w = norm2_w[l].reshape(1, D_MODEL)
    nfw = norm_f_w.reshape(1, D_MODEL)
    sub_w = subln_w[l].reshape(1, DV_A)
    gn_w = gnorm_w[l].reshape(1, DV_B)
    win = w_in[l].astype(BF16)
    wa = w_branch_a[l].astype(BF16)
    wb = w_branch_b[l].astype(BF16)
    wo = w_out[l].astype(BF16)
    wq = w_query[l].astype(BF16)
    sk = jnp.transpose(sub_keys[l].astype(BF16), (1, 0, 2, 3))
    zk = jnp.zeros_like(sk[0])
    keys = jnp.stack([jnp.concatenate([sk[0], zk], axis=-1),
                      jnp.concatenate([zk, sk[1]], axis=-1)])
    u = expert_u.reshape(expert_u.shape[1:]).astype(BF16)
    v = jnp.transpose(expert_v.reshape(expert_v.shape[1:])).astype(BF16)

    def tail(x2, oa, ob, gt, tm_merge, tm_peer, e1_per_step=16):
        h, nt, s1, s2 = _merge(x2, oa, ob, gt, wa, wb, wo, n2w, wq, keys, tm_merge)
        cnt, e1w, r2, e2w = _topk(s1, s2)
        return _peer(nt, u, v, cnt, e1w, r2, e2w, h, nfw, tm_peer, e1_per_step)


    def keys_out(kt, b, s):
        return jnp.transpose(kt.reshape(1, b, H_A, 2, DH_QK, s), (0, 1, 5, 2, 3, 4))

    t = batch * seq
    xp = x_prompt.reshape(t, D_MODEL)
    qa, ka, kt, va, vat, hg, gt = _in_proj(xp, n1w, win, batch, _tile(seq, 256))
    tq = _tile(seq, 1024)
    oa = _attn_prompt(lam, qa, ka, vat, sub_w, batch, seq, tq, _tile(tq, 1024), post_scale)
    ob, s_p = _hgrn_prompt(hg, lb, gn_w, batch, seq, _tile(seq, 128), _tile(seq, 1024))
    y_p = tail(xp, oa, ob, gt, _tile(t, 256), _tile(t, 512))

    xs = x_sample.reshape(n_dec, D_MODEL)
    qa_s, ka_s, kt_s, va_s, _, hg_s, gt_s = _in_proj(xs, n1w, win, 1, _tile(n_dec, 128))
    n_phys, page = cache_k.shape[1], cache_k.shape[2]
    cache_kt = jnp.transpose(cache_k, (0, 1, 3, 4, 5, 2)).reshape(n_phys, N_QK, page)
    cache_v2 = cache_v.reshape(n_phys, page * H_A, DV_A)
    oa_s = _attn_decode(page_table, lam, qa_s, ka_s, va_s, sub_w, cache_kt, cache_v2, post_scale)
    ob_s, s_s = _hgrn_sample(hg_s, lb, gn_w, state_hgrn.reshape(state_hgrn.shape[1:]),
                             _tile(n_dec, 8))
    y_s = tail(xs, oa_s, ob_s.reshape(n_dec, H_B * DV_B), gt_s,
               _tile(n_dec, 128), _tile(n_dec, 128))

    return (y_p.reshape(batch, seq, D_MODEL),
            y_s.reshape(n_dec, 1, D_MODEL),
            keys_out(kt, batch, seq),
            va.reshape(1, batch, seq, H_A, DV_A),
            s_p.reshape(1, batch, H_B, DK_B, DV_B),
            jnp.transpose(keys_out(kt_s, 1, n_dec), (0, 2, 1, 3, 4, 5)),
            va_s.reshape(1, n_dec, 1, H_A, DV_A),
            s_s.reshape(1, n_dec, H_B, DK_B, DV_B))
```

```python
import functools
import math

import jax
import jax.numpy as jnp
from jax import lax
from jax.experimental import pallas as pl
from jax.experimental.pallas import tpu as pltpu

F32 = jnp.float32
BF16 = jnp.bfloat16

EPS = 1e-6
NEG = -1e30
LOG2E = math.log2(math.e)
LANES = 128
SUBLANES = 8
VMEM_LIMIT = 56 * 1024 * 1024

H_A = 4
DH_QK = 64
DV_A = 128
H_B = 4
DK_B = 128
DV_B = 128
N_KEYS = 128
H_P = 8
DK_P_HALF = 64
TOPK_P = 16
D_MODEL = 1024
N_QK = H_A * 2 * DH_QK
N_HG = 4 * H_B * DK_B
N_GT = 2 * D_MODEL
N_IN = 3 * N_QK + N_HG + N_GT


def _cparams(sem):
    return pltpu.CompilerParams(dimension_semantics=sem, vmem_limit_bytes=VMEM_LIMIT)


def _rms(x, w):
    return x * lax.rsqrt(jnp.mean(x * x, axis=-1, keepdims=True) + EPS) * w


def _sigmoid(x):
    return 1.0 / (1.0 + jnp.exp(-x))


def _silu(x):
    return x * _sigmoid(x)


def _dot(a, b):
    return jnp.dot(a, b, preferred_element_type=F32)


def _dot_nt(a, b):
    return lax.dot_general(a, b, (((1,), (1,)), ((), ())), preferred_element_type=F32)


def _inproj_kernel(x_ref, g_ref, w_ref, qa_ref, ka_ref, kt_ref, va_ref, vt_ref,
                   hg_ref, gt_ref):
    nb = _rms(x_ref[...], g_ref[...]).astype(BF16)
    cw = N_QK

    def mm(c):
        return _dot(nb, w_ref[:, c * cw:(c + 1) * cw])

    qa_ref[...] = (mm(0) * (DH_QK ** -0.5 * LOG2E)).astype(BF16)
    ka = mm(1)
    ka_ref[...] = ka.astype(BF16)
    kt_ref[0] = jnp.transpose(ka)
    va = mm(2)
    for h in range(H_A):
        va_ref[pl.ds(h, va.shape[0], stride=H_A), :] = va[:, h * DV_A:(h + 1) * DV_A]
    vt_ref[0] = jnp.transpose(va).astype(BF16)
    for c in range(N_HG // cw):
        hg_ref[:, c * cw:(c + 1) * cw] = mm(3 + c)
    for c in range(N_GT // cw):
        gt_ref[:, c * cw:(c + 1) * cw] = mm(3 + N_HG // cw + c)


def _in_proj(x, g, w, batch, tm):
    m = x.shape[0]
    seq = m // batch
    nt = seq // tm
    row = lambda n: pl.BlockSpec((tm, n), lambda b, i: (b * nt + i, 0))
    full = lambda a: pl.BlockSpec(a.shape, lambda b, i: (0, 0))
    tspec = pl.BlockSpec((1, N_QK, tm), lambda b, i: (b, 0, i))
    return pl.pallas_call(
        _inproj_kernel,
        grid=(batch, nt),
        in_specs=[row(D_MODEL), full(g), full(w)],
        out_specs=[row(N_QK), row(N_QK), tspec,
                   pl.BlockSpec((tm * H_A, DV_A), lambda b, i: (b * nt + i, 0)),
                   tspec, row(N_HG), row(N_GT)],
        out_shape=[jax.ShapeDtypeStruct((m, N_QK), BF16),
                   jax.ShapeDtypeStruct((m, N_QK), BF16),
                   jax.ShapeDtypeStruct((batch, N_QK, seq), F32),
                   jax.ShapeDtypeStruct((m * H_A, DV_A), F32),
                   jax.ShapeDtypeStruct((batch, N_QK, seq), BF16),
                   jax.ShapeDtypeStruct((m, N_HG), F32),
                   jax.ShapeDtypeStruct((m, N_GT), F32)],
        compiler_params=_cparams(("parallel", "parallel")),
        name="in_proj",
    )(x, g, w)


def _attn_kernel(lam_ref, q_ref, k_ref, vt_ref, w_ref, o_ref,
                 m1, l1, a1, m2, l2, a2, *, tk, post_scale):
    qi = pl.program_id(2)
    tq = q_ref.shape[0]

    for m, l, a in ((m1, l1, a1), (m2, l2, a2)):
        m[...] = jnp.full(m.shape, NEG, F32)
        l[...] = jnp.zeros(l.shape, F32)
        a[...] = jnp.zeros(a.shape, F32)

    q = q_ref[...]
    lane = lax.broadcasted_iota(jnp.int32, q.shape, 1)
    qcs = [jnp.where((lane >= DH_QK) == bool(c), q, jnp.zeros_like(q)) for c in range(2)]

    def step(start, diag_offset):
        start = pl.multiple_of(start, tk)
        k = k_ref[pl.ds(start, tk), :]
        vt = vt_ref[0, :, pl.ds(start, tk)]
        if diag_offset is not None:
            krow = lax.broadcasted_iota(jnp.int32, (tk, tq), 0) + diag_offset
            qcol = lax.broadcasted_iota(jnp.int32, (tk, tq), 1)
            keep = krow <= qcol
        for qc, (m, l, a) in zip(qcs, ((m1, l1, a1), (m2, l2, a2))):
            s = _dot_nt(k, qc)
            if diag_offset is not None:
                s = jnp.where(keep, s, NEG)
            m_prev = m[...]
            m_new = jnp.maximum(m_prev, jnp.max(s, axis=0, keepdims=True))
            alpha = jnp.exp2(m_prev - m_new)
            p = jnp.exp2(s - m_new)
            l[...] = alpha * l[...] + jnp.sum(p, axis=0, keepdims=True)
            a[...] = alpha * a[...] + _dot(vt, p.astype(BF16))
            m[...] = m_new

    def body(ki, carry):
        step(ki * tk, None)
        return carry

    lax.fori_loop(0, qi * (tq // tk), body, 0)
    for j in range(tq // tk):
        step(qi * tq + j * tk, j * tk)

    lam = lam_ref[0]
    ot = a1[...] / l1[...] - lam * (a2[...] / l2[...])
    o_ref[...] = _rms(jnp.transpose(ot), w_ref[...]) * post_scale


def _attn_prompt(lam, qa, ka, vat, subln_w, batch, seq, tq, tk, post_scale):
    nq = seq // tq
    return pl.pallas_call(
        functools.partial(_attn_kernel, tk=tk, post_scale=post_scale),
        grid=(batch, H_A, nq),
        in_specs=[pl.BlockSpec(memory_space=pltpu.SMEM),
                  pl.BlockSpec((tq, DV_A), lambda b, h, qi: (b * nq + qi, h)),
                  pl.BlockSpec((seq, DV_A), lambda b, h, qi: (b, h)),
                  pl.BlockSpec((1, DV_A, seq), lambda b, h, qi: (b, h, 0)),
                  pl.BlockSpec((1, DV_A), lambda b, h, qi: (0, 0))],
        out_specs=pl.BlockSpec((tq, DV_A), lambda b, h, qi: (b * nq + qi, h)),
        out_shape=jax.ShapeDtypeStruct((batch * seq, H_A * DV_A), F32),
        scratch_shapes=[pltpu.VMEM((1, tq), F32), pltpu.VMEM((1, tq), F32),
                        pltpu.VMEM((DV_A, tq), F32)] * 2,
        compiler_params=_cparams(("parallel", "parallel", "arbitrary")),
        name="attn_prompt",
    )(lam, qa, ka, vat, subln_w)


def _decode_kernel(pt_ref, lam_ref, q_ref, kn_ref, vn_ref, w_ref, *refs, n_pages, post_scale):
    k_refs = refs[:n_pages]
    v_refs = refs[n_pages:2 * n_pages]
    o_ref = refs[2 * n_pages]
    del pt_ref
    nmap = 2 * H_A
    width = N_QK
    page = k_refs[0].shape[2]

    q = q_ref[0].astype(F32)
    rows = lax.broadcasted_iota(jnp.int32, (nmap, width), 0)
    lane = lax.broadcasted_iota(jnp.int32, (nmap, width), 1)
    qrows = jnp.where(lane // DH_QK == rows, jnp.broadcast_to(q, (nmap, width)), 0.0)
    qb = qrows.astype(BF16)

    s_new = jnp.sum(qrows * kn_ref[0].astype(F32), axis=-1, keepdims=True)
    s = [_dot(qb, k_refs[i][0].astype(BF16)) for i in range(n_pages)]
    m = functools.reduce(jnp.maximum, s)
    m = jnp.maximum(jnp.max(m, axis=-1, keepdims=True), s_new)
    p = [jnp.exp2(si - m) for si in s]
    p_new = jnp.exp2(s_new - m)
    l = jnp.sum(functools.reduce(jnp.add, p), axis=-1, keepdims=True) + p_new

    r8 = lax.broadcasted_iota(jnp.int32, (nmap, page), 0)
    o8 = jnp.zeros((nmap, DV_A), F32)
    for i in range(n_pages):
        pb = p[i].astype(BF16)
        for h in range(H_A):
            vh = v_refs[i][0, pl.ds(h, page, stride=H_A), :].astype(BF16)
            o8 = o8 + _dot(jnp.where(r8 // 2 == h, pb, jnp.zeros_like(pb)), vh)

    vn = vn_ref[0].astype(BF16).astype(F32)
    r8v = lax.broadcasted_iota(jnp.int32, (nmap, DV_A), 0)
    vn8 = jnp.zeros((nmap, DV_A), F32)
    for h in range(H_A):
        vn8 = jnp.where(r8v // 2 == h, jnp.broadcast_to(vn[h:h + 1, :], (nmap, DV_A)), vn8)
    o8 = (o8 + p_new.astype(BF16).astype(F32) * vn8) / l
    lam = lam_ref[0]
    w = w_ref[...]
    for h in range(H_A):
        d = o8[2 * h:2 * h + 1] - lam * o8[2 * h + 1:2 * h + 2]
        o_ref[0, :, h * DV_A:(h + 1) * DV_A] = _rms(d, w) * post_scale


def _attn_decode(page_table, lam, qa, ka, va, subln_w, cache_kt, cache_v, post_scale):
    nb, n_pages = page_table.shape
    width, page = cache_kt.shape[1], cache_kt.shape[2]
    one = lambda: pl.BlockSpec((1, 1, width), lambda b, pt: (b, 0, 0))
    kspec = lambda i: pl.BlockSpec((1, width, page), lambda b, pt: (pt[b, i], 0, 0))
    vspec = lambda i: pl.BlockSpec((1, page * H_A, DV_A), lambda b, pt: (pt[b, i], 0, 0))
    grid_spec = pltpu.PrefetchScalarGridSpec(
        num_scalar_prefetch=1,
        grid=(nb,),
        in_specs=[pl.BlockSpec(memory_space=pltpu.SMEM), one(), one(),
                  pl.BlockSpec((1, H_A, DV_A), lambda b, pt: (b, 0, 0)),
                  pl.BlockSpec((1, DV_A), lambda b, pt: (0, 0))]
                 + [kspec(i) for i in range(n_pages)] + [vspec(i) for i in range(n_pages)],
        out_specs=one(),
    )
    r3 = lambda a: a.reshape(nb, 1, width)
    out = pl.pallas_call(
        functools.partial(_decode_kernel, n_pages=n_pages, post_scale=post_scale),
        grid_spec=grid_spec,
        out_shape=jax.ShapeDtypeStruct((nb, 1, width), F32),
        compiler_params=_cparams(("arbitrary",)),
        name="attn_decode",
    )(page_table, lam, r3(qa), r3(ka), va.reshape(nb, H_A, DV_A), subln_w,
      *([cache_kt] * n_pages), *([cache_v] * n_pages))
    return out.reshape(nb, width)


def _hgrn_gates(qb, fb, lb):
    q = _silu(qb) * (DK_B ** -0.5)
    f = lb + (1.0 - lb) * _sigmoid(fb)
    return q, 1.0 - f, jnp.log(f)


def _hgrn_post(o, gb, w):
    return _rms(o, w) * _silu(gb)


def _hgrn_prompt_kernel(qb_ref, fb_ref, ib_ref, gb_ref, lb_ref, w_ref, o_ref, s_ref, st_ref, *,
                        chunk):
    c = chunk
    dk = DK_B
    seq = qb_ref.shape[0]
    heads = lb_ref.shape[0]
    gw = w_ref[...]

    row = lax.broadcasted_iota(jnp.int32, (c, c), 0)
    col = lax.broadcasted_iota(jnp.int32, (c, c), 1)
    tril = jnp.where(col <= row, 1.0, 0.0).astype(BF16)
    ones = jnp.ones((dk, c), BF16)
    levels = []
    m = SUBLANES
    while 2 * m <= c:
        levels.append((m, jnp.where(row // (2 * m) == col // (2 * m), 1.0, 0.0)))
        m *= 2
    diag = [jnp.where(col == (row // SUBLANES) * SUBLANES + s, 1.0, 0.0) for s in range(SUBLANES)]

    def one_head(ci, st, hh):
        rows = pl.ds(pl.multiple_of(ci * c, c), c)
        lanes = slice(hh * dk, (hh + 1) * dk)
        q, k, lf = _hgrn_gates(qb_ref[rows, lanes], fb_ref[rows, lanes], lb_ref[hh])
        v = ib_ref[rows, lanes]

        hi = lf.astype(BF16)
        r1 = lf - hi.astype(F32)
        mid = r1.astype(BF16)
        lo = (r1 - mid.astype(F32)).astype(BF16)
        g = _dot(tril, hi) + _dot(tril, mid) + _dot(tril, lo)

        o = _dot_nt((q * jnp.exp(g)).astype(BF16), st.astype(BF16))

        a = jnp.zeros((c, c), F32)
        for m, same_block in levels:
            nb = c // (2 * m)
            g3 = g.reshape(nb, 2 * m, dk)
            d = g3 - g3[:, m - 1:m, :]
            rin = lax.broadcasted_iota(jnp.int32, (nb, 2 * m, dk), 1)
            qs = q.reshape(nb, 2 * m, dk) * jnp.exp(jnp.where(rin >= m, d, NEG))
            ks = k.reshape(nb, 2 * m, dk) * jnp.exp(jnp.where(rin < m, -d, NEG))
            al = _dot_nt(qs.reshape(c, dk).astype(BF16), ks.reshape(c, dk).astype(BF16))
            a = a + al * same_block

        nb = c // SUBLANES
        g8 = g.reshape(nb, SUBLANES, dk)
        q8 = q.reshape(nb, SUBLANES, dk)
        k8 = k.reshape(nb, SUBLANES, dk)
        rin = lax.broadcasted_iota(jnp.int32, (nb, SUBLANES, dk), 1)
        for s in range(SUBLANES):
            d = g8 - g8[:, s:s + 1, :]
            p = q8 * k8[:, s:s + 1, :] * jnp.exp(jnp.where(rin >= s, d, NEG))
            r = _dot(p.reshape(c, dk).astype(BF16), ones)
            a = a + r * diag[s]

        o = o + _dot(a.astype(BF16), v.astype(BF16))
        o_ref[rows, lanes] = _hgrn_post(o, gb_ref[rows, lanes], gw)

        g_end = g[c - 1:c, :]
        kd = (k * jnp.exp(g_end - g)).astype(BF16)
        return st * jnp.exp(g_end) + _dot(jnp.transpose(v).astype(BF16), kd)

    def body(ci, sts):
        return tuple(one_head(ci, st, hh) for hh, st in enumerate(sts))

    sb = pl.program_id(1)

    @pl.when(sb == 0)
    def _():
        st_ref[...] = jnp.zeros(st_ref.shape, F32)

    sts = lax.fori_loop(0, seq // c, body, tuple(st_ref[hh] for hh in range(heads)))
    for hh, st in enumerate(sts):
        st_ref[hh] = st

    @pl.when(sb == pl.num_programs(1) - 1)
    def _():
        for hh, st in enumerate(sts):
            s_ref[0, hh] = jnp.transpose(st)


def _hgrn_prompt(hg, lb, gnorm_w, batch, seq, chunk, rows):
    nsb = seq // rows
    width = H_B * DK_B
    blk = lambda j: pl.BlockSpec((rows, width), lambda b, s: (b * nsb + s, j))
    return pl.pallas_call(
        functools.partial(_hgrn_prompt_kernel, chunk=chunk),
        grid=(batch, nsb),
        in_specs=[blk(0), blk(1), blk(2), blk(3),
                  pl.BlockSpec((H_B, 1, DK_B), lambda b, s: (0, 0, 0)),
                  pl.BlockSpec((1, DV_B), lambda b, s: (0, 0))],
        out_specs=[pl.BlockSpec((rows, H_B * DV_B), lambda b, s: (b * nsb + s, 0)),
                   pl.BlockSpec((1, H_B, DK_B, DV_B), lambda b, s: (b, 0, 0, 0))],
        out_shape=[jax.ShapeDtypeStruct((batch * seq, H_B * DV_B), F32),
                   jax.ShapeDtypeStruct((batch, H_B, DK_B, DV_B), F32)],
        scratch_shapes=[pltpu.VMEM((H_B, DV_B, DK_B), F32)],
        compiler_params=_cparams(("parallel", "arbitrary")),
        name="hgrn_prompt",
    )(hg, hg, hg, hg, lb, gnorm_w)


def _hgrn_sample_kernel(hg_ref, lb_ref, w_ref, s0_ref, o_ref, s_ref):
    nb = s0_ref.shape[0]
    for b in range(nb):
        for h in range(H_B):
            sl = lambda j: slice((j * H_B + h) * DK_B, (j * H_B + h + 1) * DK_B)
            row = lambda j: hg_ref[b, :, sl(j)]
            q, k, lf = _hgrn_gates(row(0), row(1), lb_ref[h])
            v = row(2)
            stack = jnp.concatenate(
                [q, k, jnp.exp(lf), jnp.zeros((DK_B - 3, DK_B), F32)], axis=0)
            cols = jnp.transpose(stack)
            s_new = cols[:, 2:3] * s0_ref[b, h] + cols[:, 1:2] * v
            s_ref[b, h] = s_new
            o = jnp.sum(cols[:, 0:1] * s_new, axis=0, keepdims=True)
            o_ref[b, :, h * DV_B:(h + 1) * DV_B] = _hgrn_post(o, row(3), w_ref[...])


def _hgrn_sample(hg, lb, gnorm_w, state, tb):
    nb = state.shape[0]
    return pl.pallas_call(
        _hgrn_sample_kernel,
        grid=(nb // tb,),
        in_specs=[pl.BlockSpec((tb, 1, N_HG), lambda i: (i, 0, 0)),
                  pl.BlockSpec((H_B, 1, DK_B), lambda i: (0, 0, 0)),
                  pl.BlockSpec((1, DV_B), lambda i: (0, 0)),
                  pl.BlockSpec((tb, H_B, DK_B, DV_B), lambda i: (i, 0, 0, 0))],
        out_specs=[pl.BlockSpec((tb, 1, H_B * DV_B), lambda i: (i, 0, 0)),
                   pl.BlockSpec((tb, H_B, DK_B, DV_B), lambda i: (i, 0, 0, 0))],
        out_shape=[jax.ShapeDtypeStruct((nb, 1, H_B * DV_B), F32),
                   jax.ShapeDtypeStruct(state.shape, F32)],
        compiler_params=_cparams(("parallel",)),
        name="hgrn_sample",
    )(hg.reshape(nb, 1, N_HG), lb, gnorm_w, state)


def _merge_kernel(x_ref, oa_ref, ob_ref, gt_ref, wa_ref, wb_ref, wo_ref, n2w_ref, wq_ref,
                  keys_ref, h_ref, nt_ref, cnt_ref, e1_ref, r2_ref, e2_ref, s1_ref, s2_ref):
    ga = gt_ref[:, :D_MODEL]
    gb = gt_ref[:, D_MODEL:]
    m = (_sigmoid(ga) * _dot(oa_ref[...].astype(BF16), wa_ref[...])
         + _sigmoid(gb) * _dot(ob_ref[...].astype(BF16), wb_ref[...]))
    h = x_ref[...] + _dot(m.astype(BF16), wo_ref[...])
    h_ref[...] = h
    n2 = _rms(h, n2w_ref[...])
    nt_ref[...] = jnp.transpose(n2).astype(BF16)
    qp = _dot(n2.astype(BF16), wq_ref[...]).astype(BF16)
    for hh in range(H_P):
        qh = qp[:, hh * LANES:(hh + 1) * LANES]
        rows = pl.ds(hh, N_KEYS, stride=H_P)
        for c, s_ref in enumerate((s1_ref, s2_ref)):
            s = _dot_nt(keys_ref[c, hh], qh)
            for blk in range(s_ref.shape[0]):
                s_ref[blk, rows, :] = s[:, blk * LANES:(blk + 1) * LANES]
    for blk in range(s1_ref.shape[0]):
        lanes = pl.ds(blk * LANES, LANES)
        _topk_kernel(s1_ref.at[pl.ds(blk, 1)], s2_ref.at[pl.ds(blk, 1)],
                     cnt_ref.at[:, lanes], e1_ref.at[:, lanes],
                     r2_ref.at[:, :, lanes], e2_ref.at[:, :, lanes])


def _merge(x, oa, ob, gt, wa, wb, wo, n2w, wq, keys, tm):
    t = x.shape[0]
    row = lambda n: pl.BlockSpec((tm, n), lambda i: (i, 0))
    full = lambda a: pl.BlockSpec(a.shape, lambda i: (0,) * a.ndim)
    kspec = pl.BlockSpec((N_KEYS * H_P, tm), lambda i: (0, i))
    hspec = pl.BlockSpec((H_P, N_KEYS, tm), lambda i: (0, 0, i))
    return pl.pallas_call(
        _merge_kernel,
        grid=(t // tm,),
        in_specs=[row(D_MODEL), row(N_QK), row(H_B * DV_B), row(N_GT),
                  full(wa), full(wb), full(wo), full(n2w), full(wq), full(keys)],
        out_specs=[row(D_MODEL), pl.BlockSpec((D_MODEL, tm), lambda i: (0, i)),
                   kspec, kspec, hspec, hspec],
        out_shape=[jax.ShapeDtypeStruct((t, D_MODEL), F32),
                   jax.ShapeDtypeStruct((D_MODEL, t), BF16),
                   jax.ShapeDtypeStruct((N_KEYS * H_P, t), F32),
                   jax.ShapeDtypeStruct((N_KEYS * H_P, t), F32),
                   jax.ShapeDtypeStruct((H_P, N_KEYS, t), BF16),
                   jax.ShapeDtypeStruct((H_P, N_KEYS, t), BF16)],
        scratch_shapes=[pltpu.VMEM((tm // LANES, N_KEYS * H_P, LANES), F32)] * 2,
        compiler_params=_cparams(("parallel",)),
        name="merge",
    )(x, oa, ob, gt, wa, wb, wo, n2w, wq, keys)


def _bitonic_merge_desc(a):
    a = list(a)
    d = len(a) // 2
    while d >= 1:
        for i in range(len(a)):
            if i & d == 0:
                a[i], a[i + d] = jnp.maximum(a[i], a[i + d]), jnp.minimum(a[i], a[i + d])
        d //= 2
    return a


def _sort_desc(a):
    if len(a) == 1:
        return list(a)
    half = len(a) // 2
    return _bitonic_merge_desc(_sort_desc(a[:half]) + _sort_desc(a[half:])[::-1])


def _top16(vals):
    groups = [_sort_desc(vals[i:i + TOPK_P]) for i in range(0, len(vals), TOPK_P)]
    while len(groups) > 1:
        groups = [_bitonic_merge_desc([jnp.maximum(a[i], b[TOPK_P - 1 - i])
                                       for i in range(TOPK_P)])
                  for a, b in zip(groups[0::2], groups[1::2])]
    return groups[0]


def _prefix_count(x, th, below):
    assert len(th) == TOPK_P == 16
    hit = (lambda t: x < t) if below else (lambda t: x >= t)
    b3 = hit(th[7])
    b2 = hit(jnp.where(b3, th[11], th[3]))
    b1 = hit(jnp.where(b3, jnp.where(b2, th[13], th[9]), jnp.where(b2, th[5], th[1])))
    lo = jnp.where(b2, jnp.where(b1, th[6], th[4]), jnp.where(b1, th[2], th[0]))
    hi = jnp.where(b2, jnp.where(b1, th[14], th[12]), jnp.where(b1, th[10], th[8]))
    b0 = hit(jnp.where(b3, hi, lo))
    bit = lambda b, v: jnp.where(b, v, 0.0)
    return bit(b3, 8.0) + bit(b2, 4.0) + bit(b1, 2.0) + bit(b0, 1.0) + bit(hit(th[15]), 1.0)


def _topk_kernel(s1_ref, s2_ref, cnt_ref, e1_ref, r2_ref, e2_ref):
    tl = s1_ref.shape[2]
    key_rows = lambda ref, e: ref[0, pl.ds(e * H_P, H_P), :]
    t1 = _top16([key_rows(s1_ref, e) for e in range(N_KEYS)])
    t2 = _top16([key_rows(s2_ref, e) for e in range(N_KEYS)])
    pairs = [(p, r) for p in range(TOPK_P) for r in range(TOPK_P) if (p + 1) * (r + 1) <= TOPK_P]
    sums = {pr: t1[pr[0]] + t2[pr[1]] for pr in pairs}
    pad = [jnp.full((H_P, tl), NEG, F32)] * (-len(pairs) % (2 * TOPK_P))
    top = _top16([sums[pr] for pr in pairs] + pad)
    tau = top[TOPK_P - 1]
    z = functools.reduce(jnp.add, [jnp.exp(c - top[0]) for c in top])
    theta = []
    for r in range(TOPK_P):
        th = jnp.full((H_P, tl), -NEG, F32)
        for p in range(TOPK_P // (r + 1)):
            th = jnp.where(sums[(p, r)] >= tau, t1[p], th)
        theta.append(th)
    shift1 = t1[0] + jnp.log(z)
    for e in range(N_KEYS):
        x = key_rows(s1_ref, e)
        cnt_ref[pl.ds(e * H_P, H_P), :] = _prefix_count(x, theta, below=False)
        e1_ref[pl.ds(e * H_P, H_P), :] = jnp.exp(x - shift1)
    for h in range(H_P):
        xh = s2_ref[0, pl.ds(h, N_KEYS, stride=H_P), :]
        rank = _prefix_count(xh, [t[h:h + 1, :] for t in t2], below=True)
        r2_ref[h] = rank.astype(BF16)
        e2_ref[h] = jnp.exp(xh - t2[0][h:h + 1, :]).astype(BF16)


def _topk(s1, s2):
    tl = s1.shape[2]
    t = s1.shape[0] * tl
    sspec = pl.BlockSpec((N_KEYS * H_P, tl), lambda i: (0, i))
    hspec = pl.BlockSpec((H_P, N_KEYS, tl), lambda i: (0, 0, i))
    ispec = pl.BlockSpec((1, N_KEYS * H_P, tl), lambda i: (i, 0, 0))
    return pl.pallas_call(
        _topk_kernel,
        grid=(t // tl,),
        in_specs=[ispec, ispec],
        out_specs=[sspec, sspec, hspec, hspec],
        out_shape=[jax.ShapeDtypeStruct((N_KEYS * H_P, t), F32),
                   jax.ShapeDtypeStruct((N_KEYS * H_P, t), F32),
                   jax.ShapeDtypeStruct((H_P, N_KEYS, t), BF16),
                   jax.ShapeDtypeStruct((H_P, N_KEYS, t), BF16)],
        compiler_params=_cparams(("parallel",)),
        name="topk",
    )(s1, s2)


def _gelu(x):
    return 0.5 * x * (1.0 + lax.erf(x * (2.0 ** -0.5)))


def _peer_kernel(xt_ref, u_ref, vt_ref, cnt_ref, e1_ref, r2_ref, e2_ref, h_ref, nfw_ref,
                 y_ref, acc_ref, *, e1_per_step):
    j = pl.program_id(1)
    nj = pl.num_programs(1)
    tm = xt_ref.shape[1]
    pk = 2 * SUBLANES

    def row_tile(ref, row):
        return jnp.broadcast_to(ref[row, :], (pk, tm)).astype(BF16)[None]

    def weights(e1):
        w = None
        for hh in range(H_P):
            row = pl.ds(e1 * H_P + hh, 1)
            cnt = row_tile(cnt_ref, row)
            e2 = e2_ref[hh].reshape(N_KEYS // pk, pk, tm)
            r2 = r2_ref[hh].reshape(N_KEYS // pk, pk, tm)
            term = jnp.where(r2 < cnt, e2, jnp.zeros_like(e2)) * row_tile(e1_ref, row)
            w = term if w is None else w + term
        return w.reshape(N_KEYS, tm)

    @pl.when(j == 0)
    def _():
        acc_ref[...] = jnp.zeros(acc_ref.shape, F32)

    w = jnp.concatenate([weights(j * e1_per_step + a) for a in range(e1_per_step)], axis=0)
    act = _gelu(_dot(u_ref[...], xt_ref[...]))
    acc_ref[...] += _dot(vt_ref[...], w * act.astype(BF16))

    @pl.when(j == nj - 1)
    def _():
        y_ref[...] = _rms(h_ref[...] + jnp.transpose(acc_ref[...]), nfw_ref[...])


def _peer(xt, u, v, cnt, e1w, r2, e2w, h, nfw, tm, e1_per_step):
    t = h.shape[0]
    te = e1_per_step * N_KEYS
    kspec = pl.BlockSpec((N_KEYS * H_P, tm), lambda i, j: (0, i))
    hspec = pl.BlockSpec((H_P, N_KEYS, tm), lambda i, j: (0, 0, i))
    row = pl.BlockSpec((tm, D_MODEL), lambda i, j: (i, 0))
    return pl.pallas_call(
        functools.partial(_peer_kernel, e1_per_step=e1_per_step),
        grid=(t // tm, N_KEYS // e1_per_step),
        in_specs=[pl.BlockSpec((D_MODEL, tm), lambda i, j: (0, i)),
                  pl.BlockSpec((te, D_MODEL), lambda i, j: (j, 0)),
                  pl.BlockSpec((D_MODEL, te), lambda i, j: (0, j)),
                  kspec, kspec, hspec, hspec, row,
                  pl.BlockSpec((1, D_MODEL), lambda i, j: (0, 0))],
        out_specs=row,
        out_shape=jax.ShapeDtypeStruct((t, D_MODEL), F32),
        scratch_shapes=[pltpu.VMEM((D_MODEL, tm), F32)],
        compiler_params=_cparams(("parallel", "arbitrary")),
        name="peer",
    )(xt, u, v, cnt, e1w, r2, e2w, h, nfw)


def _tile(n, pref):
    t = min(n, pref)
    assert n % t == 0, (n, t)
    return t


def kernel(x_prompt, x_sample, cache_k, cache_v, state_hgrn, page_table, norm1_w, w_in,
           lambda_q1, lambda_k1, lambda_q2, lambda_k2, subln_w, lb_param, gnorm_w,
           w_branch_a, w_branch_b, w_out, norm2_w, w_query, sub_keys, expert_u, expert_v,
           norm_f_w):
    depth = w_in.shape[0]
    assert depth == 1 and w_in.shape[2] == N_IN
    batch, seq, _ = x_prompt.shape
    n_dec, dec_len, _ = x_sample.shape
    assert dec_len == 1
    l = 0

    lam_init = 0.8 - 0.6 * math.exp(-0.3 * l)
    post_scale = 1.0 - lam_init
    f = F32
    lam = (jnp.exp(jnp.sum(lambda_q1[l].astype(f) * lambda_k1[l].astype(f)))
           - jnp.exp(jnp.sum(lambda_q2[l].astype(f) * lambda_k2[l].astype(f)))
           + lam_init).reshape(1)
    lb = jnp.cumsum(jax.nn.softmax(lb_param.astype(f), axis=0), axis=0)[l].reshape(H_B, 1, DK_B)

    n1w = norm1_w[l].reshape(1, D_MODEL)
    n2w = norm2_w[l].reshape(1, D_MODEL)
    nfw = norm_f_w.reshape(1, D_MODEL)
    sub_w = subln_w[l].reshape(1, DV_A)
    gn_w = gnorm_w[l].reshape(1, DV_B)
    win = w_in[l].astype(BF16)
    wa = w_branch_a[l].astype(BF16)
    wb = w_branch_b[l].astype(BF16)
    wo = w_out[l].astype(BF16)
    wq = w_query[l].astype(BF16)
    sk = jnp.transpose(sub_keys[l].astype(BF16), (1, 0, 2, 3))
    zk = jnp.zeros_like(sk[0])
    keys = jnp.stack([jnp.concatenate([sk[0], zk], axis=-1),
                      jnp.concatenate([zk, sk[1]], axis=-1)])
    u = expert_u.reshape(expert_u.shape[1:]).astype(BF16)
    v = jnp.transpose(expert_v.reshape(expert_v.shape[1:])).astype(BF16)

    def tail(x2, oa, ob, gt, tm_merge, tm_peer, e1_per_step=16):
        h, nt, cnt, e1w, r2, e2w = _merge(x2, oa, ob, gt, wa, wb, wo, n2w, wq, keys, tm_merge)
        return _peer(nt, u, v, cnt, e1w, r2, e2w, h, nfw, tm_peer, e1_per_step)


    def keys_out(kt, b, s):
        return jnp.transpose(kt.reshape(1, b, H_A, 2, DH_QK, s), (0, 1, 5, 2, 3, 4))

    t = batch * seq
    xp = x_prompt.reshape(t, D_MODEL)
    qa, ka, kt, va, vat, hg, gt = _in_proj(xp, n1w, win, batch, _tile(seq, 256))
    tq = _tile(seq, 1024)
    oa = _attn_prompt(lam, qa, ka, vat, sub_w, batch, seq, tq, _tile(tq, 1024), post_scale)
    ob, s_p = _hgrn_prompt(hg, lb, gn_w, batch, seq, _tile(seq, 128), _tile(seq, 1024))
    y_p = tail(xp, oa, ob, gt, _tile(t, 256), _tile(t, 512))

    xs = x_sample.reshape(n_dec, D_MODEL)
    qa_s, ka_s, kt_s, va_s, _, hg_s, gt_s = _in_proj(xs, n1w, win, 1, _tile(n_dec, 128))
    n_phys, page = cache_k.shape[1], cache_k.shape[2]
    cache_kt = jnp.transpose(cache_k, (0, 1, 3, 4, 5, 2)).reshape(n_phys, N_QK, page)
    cache_v2 = cache_v.reshape(n_phys, page * H_A, DV_A)
    oa_s = _attn_decode(page_table, lam, qa_s, ka_s, va_s, sub_w, cache_kt, cache_v2, post_scale)
    ob_s, s_s = _hgrn_sample(hg_s, lb, gn_w, state_hgrn.reshape(state_hgrn.shape[1:]),
                             _tile(n_dec, 8))
    y_s = tail(xs, oa_s, ob_s.reshape(n_dec, H_B * DV_B), gt_s,
               _tile(n_dec, 128), _tile(n_dec, 128))

    return (y_p.reshape(batch, seq, D_MODEL),
            y_s.reshape(n_dec, 1, D_MODEL),
            keys_out(kt, batch, seq),
            va.reshape(1, batch, seq, H_A, DV_A),
            s_p.reshape(1, batch, H_B, DK_B, DV_B),
            jnp.transpose(keys_out(kt_s, 1, n_dec), (0, 2, 1, 3, 4, 5)),
            va_s.reshape(1, n_dec, 1, H_A, DV_A),
            s_s.reshape(1, n_dec, H_B, DK_B, DV_B))
```
